```python
import math
import jax, jax.numpy as jnp
from jax import lax
import numpy as np

D_MODEL = 1024
BATCH = 16
SEQ = 2048
DEPTH = 1

CHUNK = 64
N_MEM = 256
D_CONV = D_MODEL
CONV_WIDTH = 3
FOX_HEAD_DIM = 128
FOX_HEADS = D_MODEL // FOX_HEAD_DIM
FOX_WIDTH = FOX_HEADS * FOX_HEAD_DIM
XA_HEADS = 4
XA_HEAD_DIM = D_MODEL // XA_HEADS
XA_WIDTH = XA_HEADS * XA_HEAD_DIM
N_BRANCH = 3
D_FF = -(-8 * D_MODEL // (3 * 256)) * 256
Q_BLOCK = 128
EPS = 1e-6
IN_SPLITS = (D_CONV, D_CONV, D_CONV, FOX_WIDTH, FOX_WIDTH, FOX_WIDTH, XA_WIDTH,
             D_MODEL, D_MODEL, D_MODEL, FOX_HEADS)
IN_COLS = sum(IN_SPLITS)

kernel_name = "hybrid_conv_fox_memory_block"


def rms_norm(x, g):
    xf = x.astype(jnp.float32)
    y = xf * lax.rsqrt(jnp.mean(xf * xf, axis=-1, keepdims=True) + EPS)
    return (y * g.astype(jnp.float32)).astype(x.dtype)


def split_cols(z):
    outs, off = [], 0
    for w in IN_SPLITS:
        outs.append(z[..., off:off + w])
        off += w
    return outs


def short_gated_conv(b_gate, c_gate, v, conv_w, conv_b):
    u = c_gate * v
    rhs = conv_w[:, None, :]
    y = lax.conv_general_dilated(u, rhs, window_strides=(1,),
                                 padding=[(CONV_WIDTH - 1, 0)],
                                 dimension_numbers=('NWC', 'WIO', 'NWC'),
                                 feature_group_count=D_CONV)
    return b_gate * (y + conv_b)


def forgetting_attention(q, k, v, log_f):
    b, s, h, hd = q.shape
    nb = s // Q_BLOCK
    c = jnp.cumsum(log_f, axis=1).transpose(0, 2, 1)
    kh = k.transpose(0, 2, 1, 3)
    vh = v.transpose(0, 2, 1, 3)
    q_blocks = q.transpose(0, 2, 1, 3).reshape(b, h, nb, Q_BLOCK, hd).transpose(2, 0, 1, 3, 4)
    c_blocks = c.reshape(b, h, nb, Q_BLOCK).transpose(2, 0, 1, 3)
    k_pos = jnp.arange(s)
    scale = 1.0 / math.sqrt(hd)

    def block(args):
        qb, cb, i = args
        logits = jnp.einsum('bhqd,bhkd->bhqk', qb, kh,
                            preferred_element_type=jnp.float32) * scale
        logits = logits + cb[..., None] - c[:, :, None, :]
        q_pos = i * Q_BLOCK + jnp.arange(Q_BLOCK)
        mask = q_pos[:, None] >= k_pos[None, :]
        logits = jnp.where(mask, logits, -jnp.inf)
        p = jax.nn.softmax(logits, axis=-1).astype(vh.dtype)
        return jnp.einsum('bhqk,bhkd->bhqd', p, vh)

    out = lax.map(block, (q_blocks, c_blocks, jnp.arange(nb)))
    return out.transpose(1, 0, 3, 2, 4).reshape(b, s, h * hd)


def memory_cross_attention(q, mem_n, w_mem_kv, q_g, k_g):
    b, s = q.shape[0], q.shape[1]
    kv = mem_n @ w_mem_kv
    k = kv[..., :XA_WIDTH].reshape(b, -1, XA_HEADS, XA_HEAD_DIM)
    v = kv[..., XA_WIDTH:].reshape(b, -1, XA_HEADS, XA_HEAD_DIM)
    q = rms_norm(q, q_g)
    k = rms_norm(k, k_g)
    logits = jnp.einsum('bshd,bmhd->bhsm', q, k,
                        preferred_element_type=jnp.float32) / math.sqrt(XA_HEAD_DIM)
    p = jax.nn.softmax(logits, axis=-1).astype(v.dtype)
    return jnp.einsum('bhsm,bmhd->bshd', p, v).reshape(b, s, XA_WIDTH)


def setup_inputs(seed: int = 0) -> dict:
    key = jax.random.key(seed)
    ks = jax.random.split(key, 24)
    f32 = jnp.float32
    nrm = lambda k, shape, fan: jax.random.normal(k, shape, f32) * fan ** -0.5
    gain = lambda k, shape: 1.0 + 0.02 * jax.random.normal(k, shape, f32)
    L = DEPTH
    return {
        "x": jax.random.normal(ks[0], (BATCH, SEQ, D_MODEL), f32),
        "mem": jax.random.normal(ks[1], (BATCH, N_MEM, D_MODEL), f32),
        "norm1_g": gain(ks[2], (L, D_MODEL)),
        "w_in": nrm(ks[3], (L, D_MODEL, IN_COLS), D_MODEL),
        "conv_w": nrm(ks[4], (L, CONV_WIDTH, D_CONV), CONV_WIDTH),
        "conv_b": 0.02 * jax.random.normal(ks[5], (L, D_CONV), f32),
        "fox_f_bias": 2.0 + 2.0 * jax.random.uniform(ks[6], (L, FOX_HEADS), f32),
        "fox_q_g": gain(ks[7], (L, FOX_HEAD_DIM)),
        "fox_k_g": gain(ks[8], (L, FOX_HEAD_DIM)),
        "mem_norm_g": gain(ks[9], (L, D_MODEL)),
        "w_mem_kv": nrm(ks[10], (L, D_MODEL, 2 * XA_WIDTH), D_MODEL),
        "xa_q_g": gain(ks[11], (L, XA_HEAD_DIM)),
        "xa_k_g": gain(ks[12], (L, XA_HEAD_DIM)),
        "w_br_conv": nrm(ks[13], (L, D_CONV, D_MODEL), D_CONV),
        "w_br_fox": nrm(ks[14], (L, FOX_WIDTH, D_MODEL), FOX_WIDTH),
        "w_br_xa": nrm(ks[15], (L, XA_WIDTH, D_MODEL), XA_WIDTH),
        "w_o": nrm(ks[16], (L, D_MODEL, D_MODEL), D_MODEL),
        "norm2_g": gain(ks[17], (L, D_MODEL)),
        "w_ffn_in": nrm(ks[18], (L, D_MODEL, 2 * D_FF), D_MODEL),
        "w_ffn_out": nrm(ks[19], (L, D_FF, D_MODEL), D_FF),
    }


def reference(x, mem, norm1_g, w_in, conv_w, conv_b, fox_f_bias, fox_q_g, fox_k_g,
              mem_norm_g, w_mem_kv, xa_q_g, xa_k_g, w_br_conv, w_br_fox, w_br_xa,
              w_o, norm2_g, w_ffn_in, w_ffn_out):
    b, s, _ = x.shape
    for l in range(DEPTH):
        h = rms_norm(x, norm1_g[l])
        z = h @ w_in[l]
        (cb, cc, cv, fq, fk, fv, xq, ga, gb, gc, ff) = split_cols(z)

        y_conv = short_gated_conv(cb, cc, cv, conv_w[l], conv_b[l])

        fq = rms_norm(fq.reshape(b, s, FOX_HEADS, FOX_HEAD_DIM), fox_q_g[l])
        fk = rms_norm(fk.reshape(b, s, FOX_HEADS, FOX_HEAD_DIM), fox_k_g[l])
        fv = fv.reshape(b, s, FOX_HEADS, FOX_HEAD_DIM)
        log_f = jax.nn.log_sigmoid(ff.astype(jnp.float32) + fox_f_bias[l].astype(jnp.float32))
        y_fox = forgetting_attention(fq, fk, fv, log_f)

        mem_n = rms_norm(mem, mem_norm_g[l])
        y_xa = memory_cross_attention(xq.reshape(b, s, XA_HEADS, XA_HEAD_DIM), mem_n,
                                      w_mem_kv[l], xa_q_g[l], xa_k_g[l])

        merged = (jax.nn.sigmoid(ga) * (y_conv @ w_br_conv[l])
                  + jax.nn.sigmoid(gb) * (y_fox @ w_br_fox[l])
                  + jax.nn.sigmoid(gc) * (y_xa @ w_br_xa[l]))
        x = x + merged @ w_o[l]

        h2 = rms_norm(x, norm2_g[l])
        gu = h2 @ w_ffn_in[l]
        x = x + (jax.nn.silu(gu[..., :D_FF]) * gu[..., D_FF:]) @ w_ffn_out[l]
    return x
```

```python
import functools
import math

import jax
import jax.numpy as jnp
from jax import lax
from jax.experimental import pallas as pl
from jax.experimental.pallas import tpu as pltpu

F32 = jnp.float32
BF16 = jnp.bfloat16

D_MODEL = 1024
N_GROUPS = 10
FOX_HEADS = 8
FOX_HEAD_DIM = 128
XA_HEADS = 4
XA_HEAD_DIM = 256
D_FF = 2816
FF_CHUNK = 256
CONV_WIDTH = 3
EPS = 1e-6
LANES = 128
HALO_ROWS = 16

G_CB, G_CC, G_CV, G_FQ, G_FK, G_FV, G_XQ, G_GA, G_GB, G_GC = range(N_GROUPS)

TM_IN = 1024
TQ = 512
TK = 512
TM_BR = 512
TM_FFN = 512
VMEM_LIMIT = 56 * 1024 * 1024


def _rms(x, g):
    ms = jnp.mean(x * x, axis=-1, keepdims=True)
    return x * lax.rsqrt(ms + EPS) * g


def _dot(a, b):
    return jnp.dot(a, b, preferred_element_type=F32)


def _dot_nt(a, b):
    return lax.dot_general(a, b, (((1,), (1,)), ((), ())), preferred_element_type=F32)


def _const_spec(shape):
    nd = len(shape)
    return pl.BlockSpec(shape, lambda *_: (0,) * nd, pipeline_mode=pl.Buffered(1))


def _mem_kv_kernel(mem_ref, g_ref, w_ref, kg_ref, k_ref, v_ref):
    mn = _rms(mem_ref[0], g_ref[...]).astype(BF16)
    kv = _dot(mn, w_ref[...])
    for h in range(XA_HEADS):
        hs = slice(h * XA_HEAD_DIM, (h + 1) * XA_HEAD_DIM)
        k_ref[0, :, hs] = _rms(kv[:, hs], kg_ref[...]).astype(BF16)
    v_ref[0] = kv[:, D_MODEL:].astype(BF16)


def _mem_kv(mem, g, w, kg):
    b, m, d = mem.shape
    return pl.pallas_call(
        _mem_kv_kernel,
        grid=(b,),
        in_specs=[
            pl.BlockSpec((1, m, d), lambda i: (i, 0, 0)),
            _const_spec((1, d)),
            _const_spec((d, 2 * d)),
            _const_spec((1, XA_HEAD_DIM)),
        ],
        out_specs=[
            pl.BlockSpec((1, m, d), lambda i: (i, 0, 0)),
            pl.BlockSpec((1, m, d), lambda i: (i, 0, 0)),
        ],
        out_shape=[jax.ShapeDtypeStruct((b, m, d), BF16)] * 2,
        compiler_params=pltpu.CompilerParams(
            dimension_semantics=("arbitrary",), vmem_limit_bytes=VMEM_LIMIT),
        name="mem_kv",
    )(mem, g, w, kg)


def _log_sigmoid(x):
    return jnp.minimum(x, 0.0) - jnp.log1p(jnp.exp(-jnp.abs(x)))


def _cumsum_rows(x):
    n = x.shape[0]
    row = lax.broadcasted_iota(jnp.int32, x.shape, 0)
    s = 1
    while s < n:
        x = x + jnp.where(row >= s, pltpu.roll(x, s, 0), 0.0)
        s *= 2
    return x


def _in_proj_kernel(tiles_per_seq, x_ref, g1_ref, w_ref, wff_ref, fb_ref, qg_ref, kg_ref,
                    z_ref, ctok_ref, ct_ref, h_ref, carry_ref):
    i = pl.program_id(0)
    j = pl.program_id(1)
    tm = x_ref.shape[0]

    @pl.when(j == 0)
    def _():
        hb = _rms(x_ref[...], g1_ref[...]).astype(BF16)
        h_ref[...] = hb
        lf = _log_sigmoid(_dot(hb, wff_ref[...]) + fb_ref[...])

        @pl.when(i % tiles_per_seq == 0)
        def _():
            carry_ref[...] = jnp.zeros_like(carry_ref)

        c = _cumsum_rows(lf) + carry_ref[...]
        carry_ref[...] = c[tm - 1:tm, :]
        ctok_ref[...] = c
        ct = c.T
        for r in range(tm // TK):
            ct_ref[r] = ct[:FOX_HEADS, r * TK:(r + 1) * TK]

    @pl.when((j != G_FQ) & (j != G_FK))
    def _():
        z_ref[...] = _dot(h_ref[...], w_ref[...]).astype(BF16)

    def head_norm(g_ref, scale):
        r = _dot(h_ref[...], w_ref[...])
        for h in range(FOX_HEADS):
            hs = slice(h * FOX_HEAD_DIM, (h + 1) * FOX_HEAD_DIM)
            z_ref[:, hs] = (_rms(r[:, hs], g_ref[...]) * scale).astype(BF16)

    @pl.when(j == G_FQ)
    def _():
        head_norm(qg_ref, 1.0 / math.sqrt(FOX_HEAD_DIM))

    @pl.when(j == G_FK)
    def _():
        head_norm(kg_ref, 1.0)


def _in_proj(x2, g1, w, wff, fb, qg, kg, seq):
    n, d = x2.shape
    tm = TM_IN
    nt = n // tm
    kern = functools.partial(_in_proj_kernel, seq // tm)
    return pl.pallas_call(
        kern,
        grid=(nt, N_GROUPS),
        in_specs=[
            pl.BlockSpec((tm, d), lambda i, j: (i, 0)),
            _const_spec((1, d)),
            pl.BlockSpec((d, d), lambda i, j: (0, j)),
            _const_spec((d, LANES)),
            _const_spec((1, LANES)),
            _const_spec((1, FOX_HEAD_DIM)),
            _const_spec((1, FOX_HEAD_DIM)),
        ],
        out_specs=[
            pl.BlockSpec((tm, d), lambda i, j: (i, j)),
            pl.BlockSpec((tm, LANES), lambda i, j: (i, 0)),
            pl.BlockSpec((tm // TK, FOX_HEADS, TK), lambda i, j: (i, 0, 0)),
        ],
        out_shape=[
            jax.ShapeDtypeStruct((n, N_GROUPS * d), BF16),
            jax.ShapeDtypeStruct((n, LANES), F32),
            jax.ShapeDtypeStruct((n // TK, FOX_HEADS, TK), F32),
        ],
        scratch_shapes=[
            pltpu.VMEM((tm, d), BF16),
            pltpu.VMEM((1, LANES), F32),
        ],
        compiler_params=pltpu.CompilerParams(
            dimension_semantics=("arbitrary", "arbitrary"), vmem_limit_bytes=VMEM_LIMIT),
        name="in_proj",
    )(x2, g1, w, wff, fb, qg, kg)


def _fox_kernel(q_ref, k_ref, v_ref, cq_ref, ct_ref, o_ref, m_ref, l_ref, acc_ref):
    qi = pl.program_id(1)
    row = lax.broadcasted_iota(jnp.int32, (TQ, TK), 0)
    col = lax.broadcasted_iota(jnp.int32, (TQ, TK), 1)
    causal = col <= row

    for h in range(FOX_HEADS):
        hs = slice(h * FOX_HEAD_DIM, (h + 1) * FOX_HEAD_DIM)
        q = q_ref[:, hs]
        cq = cq_ref[:, h:h + 1]
        m_ref[...] = jnp.full_like(m_ref, -jnp.inf)
        l_ref[...] = jnp.zeros_like(l_ref)
        acc_ref[...] = jnp.zeros_like(acc_ref)

        def step(kb, masked):
            off = pl.multiple_of(kb * TK, TK)
            k = k_ref[pl.ds(off, TK), hs]
            v = v_ref[pl.ds(off, TK), hs]
            ck = ct_ref[kb, h:h + 1, :]
            t = _dot_nt(q, k) - ck
            if masked:
                t = jnp.where(causal, t, -jnp.inf)
            m_old = m_ref[...]
            m_new = jnp.maximum(m_old, jnp.max(t, axis=-1, keepdims=True) + cq)
            p = jnp.exp(t - (m_new - cq))
            alpha = jnp.exp(m_old - m_new)
            l_ref[...] = alpha * l_ref[...] + jnp.sum(p, axis=-1, keepdims=True)
            acc_ref[...] = alpha * acc_ref[...] + _dot(p.astype(BF16), v)
            m_ref[...] = m_new

        def body(kb, _):
            step(kb, False)
            return 0

        lax.fori_loop(0, qi, body, 0)
        step(qi, True)
        o_ref[:, hs] = (acc_ref[...] / l_ref[...]).astype(BF16)


def _fox(z, ctok, ct, batch, seq):
    n = z.shape[0]
    d = D_MODEL
    nq = seq // TQ
    nk = seq // TK
    return pl.pallas_call(
        _fox_kernel,
        grid=(batch, nq),
        in_specs=[
            pl.BlockSpec((TQ, d), lambda b, q: (b * nq + q, G_FQ)),
            pl.BlockSpec((seq, d), lambda b, q: (b, G_FK)),
            pl.BlockSpec((seq, d), lambda b, q: (b, G_FV)),
            pl.BlockSpec((TQ, LANES), lambda b, q: (b * nq + q, 0)),
            pl.BlockSpec((nk, FOX_HEADS, TK), lambda b, q: (b, 0, 0)),
        ],
        out_specs=pl.BlockSpec((TQ, d), lambda b, q: (b * nq + q, 0)),
        out_shape=jax.ShapeDtypeStruct((n, d), BF16),
        scratch_shapes=[
            pltpu.VMEM((TQ, 1), F32),
            pltpu.VMEM((TQ, 1), F32),
            pltpu.VMEM((TQ, FOX_HEAD_DIM), F32),
        ],
        compiler_params=pltpu.CompilerParams(
            dimension_semantics=("arbitrary", "arbitrary"), vmem_limit_bytes=VMEM_LIMIT),
        name="fox",
    )(z, z, z, ctok, ct)


def _branch_kernel(tiles_per_seq, cb_ref, cc_ref, cv_ref, ccp_ref, cvp_ref, xq_ref,
                   ga_ref, gb_ref, gc_ref, yf_ref, x_ref, mk_ref, mv_ref,
                   cw_ref, cbias_ref, xqg_ref, wa_ref, wb_ref, wc_ref, wo_ref, o_ref):
    i = pl.program_id(0)
    tm = x_ref.shape[0]

    u = cc_ref[...].astype(F32) * cv_ref[...].astype(F32)
    first = (i % tiles_per_seq) == 0
    prev = ccp_ref[...].astype(F32) * cvp_ref[...].astype(F32)
    prev = jnp.where(first, 0.0, prev)
    hrow = lax.broadcasted_iota(jnp.int32, (HALO_ROWS, D_MODEL), 0)
    u1 = pltpu.roll(u, 1, 0)
    u2 = pltpu.roll(u, 2, 0)
    u1 = jnp.concatenate(
        [jnp.where(hrow < 1, pltpu.roll(prev, 1, 0), u1[:HALO_ROWS]), u1[HALO_ROWS:]], axis=0)
    u2 = jnp.concatenate(
        [jnp.where(hrow < 2, pltpu.roll(prev, 2, 0), u2[:HALO_ROWS]), u2[HALO_ROWS:]], axis=0)
    conv = cw_ref[0:1, :] * u2 + cw_ref[1:2, :] * u1 + cw_ref[2:3, :] * u
    y_conv = cb_ref[...].astype(F32) * (conv + cbias_ref[...])
    merged = jax.nn.sigmoid(ga_ref[...].astype(F32)) * _dot(y_conv.astype(BF16), wa_ref[...])

    merged += jax.nn.sigmoid(gb_ref[...].astype(F32)) * _dot(yf_ref[...], wb_ref[...])

    heads = []
    for h in range(XA_HEADS):
        hs = slice(h * XA_HEAD_DIM, (h + 1) * XA_HEAD_DIM)
        q = _rms(xq_ref[:, hs].astype(F32), xqg_ref[...]) * (1.0 / math.sqrt(XA_HEAD_DIM))
        s = _dot_nt(q.astype(BF16), mk_ref[0, :, hs])
        p = jnp.exp(s - jnp.max(s, axis=-1, keepdims=True))
        o = _dot(p.astype(BF16), mv_ref[0, :, hs])
        heads.append(o / jnp.sum(p, axis=-1, keepdims=True))
    y_xa = jnp.concatenate(heads, axis=-1).astype(BF16)
    merged += jax.nn.sigmoid(gc_ref[...].astype(F32)) * _dot(y_xa, wc_ref[...])

    o_ref[...] = x_ref[...] + _dot(merged.astype(BF16), wo_ref[...])


def _branch(z, y_fox, x2, mk, mv, cw, cbias, xqg, wa, wb, wc, wo, seq):
    n, d = x2.shape
    tm = TM_BR
    nt = n // tm
    tps = seq // tm
    n_mem = mk.shape[1]
    hpt = tm // HALO_ROWS

    def zspec(g):
        return pl.BlockSpec((tm, d), lambda i: (i, g))

    def halo(g):
        return pl.BlockSpec((HALO_ROWS, d), lambda i: (jnp.maximum(i * hpt - 1, 0), g))

    kern = functools.partial(_branch_kernel, tps)
    return pl.pallas_call(
        kern,
        grid=(nt,),
        in_specs=[
            zspec(G_CB), zspec(G_CC), zspec(G_CV), halo(G_CC), halo(G_CV), zspec(G_XQ),
            zspec(G_GA), zspec(G_GB), zspec(G_GC),
            pl.BlockSpec((tm, d), lambda i: (i, 0)),
            pl.BlockSpec((tm, d), lambda i: (i, 0)),
            pl.BlockSpec((1, n_mem, d), lambda i: (i // tps, 0, 0)),
            pl.BlockSpec((1, n_mem, d), lambda i: (i // tps, 0, 0)),
            _const_spec((CONV_WIDTH, d)),
            _const_spec((1, d)),
            _const_spec((1, XA_HEAD_DIM)),
            _const_spec((d, d)), _const_spec((d, d)), _const_spec((d, d)), _const_spec((d, d)),
        ],
        out_specs=pl.BlockSpec((tm, d), lambda i: (i, 0)),
        out_shape=jax.ShapeDtypeStruct((n, d), F32),
        compiler_params=pltpu.CompilerParams(
            dimension_semantics=("arbitrary",), vmem_limit_bytes=VMEM_LIMIT),
        name="branch",
    )(z, z, z, z, z, z, z, z, z, y_fox, x2, mk, mv, cw, cbias, xqg, wa, wb, wc, wo)


def _ffn_kernel(x_ref, g_ref, wg_ref, wu_ref, wo_ref, o_ref, h_ref, acc_ref):
    x = x_ref[...]
    h_ref[...] = _rms(x, g_ref[...]).astype(BF16)
    acc_ref[...] = x

    def body(c, _):
        h = h_ref[...]
        g = _dot(h, wg_ref[c])
        u = _dot(h, wu_ref[c])
        a = (g * jax.nn.sigmoid(g) * u).astype(BF16)
        acc_ref[...] += _dot(a, wo_ref[c])
        return 0

    lax.fori_loop(0, wg_ref.shape[0], body, 0)
    o_ref[...] = acc_ref[...]


def _ffn(x1, g, wg, wu, wo):
    n, d = x1.shape
    tm = TM_FFN
    nc = wg.shape[0]
    return pl.pallas_call(
        _ffn_kernel,
        grid=(n // tm,),
        in_specs=[
            pl.BlockSpec((tm, d), lambda i: (i, 0)),
            _const_spec((1, d)),
            _const_spec((nc, d, FF_CHUNK)),
            _const_spec((nc, d, FF_CHUNK)),
            _const_spec((nc, FF_CHUNK, d)),
        ],
        out_specs=pl.BlockSpec((tm, d), lambda i: (i, 0)),
        out_shape=jax.ShapeDtypeStruct((n, d), F32),
        scratch_shapes=[
            pltpu.VMEM((tm, d), BF16),
            pltpu.VMEM((tm, d), F32),
        ],
        compiler_params=pltpu.CompilerParams(
            dimension_semantics=("arbitrary",), vmem_limit_bytes=VMEM_LIMIT),
        name="ffn",
    )(x1, g, wg, wu, wo)


def kernel(x, mem, norm1_g, w_in, conv_w, conv_b, fox_f_bias, fox_q_g, fox_k_g, mem_norm_g,
           w_mem_kv, xa_q_g, xa_k_g, w_br_conv, w_br_fox, w_br_xa, w_o, norm2_g, w_ffn_in,
           w_ffn_out):
    batch, seq, d = x.shape
    depth = norm1_g.shape[0]
    n_main = N_GROUPS * d
    nc = D_FF // FF_CHUNK
    x2 = x.reshape(batch * seq, d)
    for l in range(depth):
        w_main = w_in[l, :, :n_main].astype(BF16)
        w_ff = jnp.pad(w_in[l, :, n_main:], ((0, 0), (0, LANES - FOX_HEADS))).astype(BF16)
        f_bias = jnp.pad(fox_f_bias[l], (0, LANES - FOX_HEADS)).reshape(1, LANES)
        wg = w_ffn_in[l, :, :D_FF].astype(BF16).reshape(d, nc, FF_CHUNK).transpose(1, 0, 2)
        wu = w_ffn_in[l, :, D_FF:].astype(BF16).reshape(d, nc, FF_CHUNK).transpose(1, 0, 2)
        wo_ffn = w_ffn_out[l].astype(BF16).reshape(nc, FF_CHUNK, d)

        mk, mv = _mem_kv(mem, mem_norm_g[l].reshape(1, d), w_mem_kv[l].astype(BF16),
                         xa_k_g[l].reshape(1, XA_HEAD_DIM))
        z, ctok, ct = _in_proj(x2, norm1_g[l].reshape(1, d), w_main, w_ff, f_bias,
                               fox_q_g[l].reshape(1, FOX_HEAD_DIM),
                               fox_k_g[l].reshape(1, FOX_HEAD_DIM), seq)
        y_fox = _fox(z, ctok, ct, batch, seq)
        x1 = _branch(z, y_fox, x2, mk, mv, conv_w[l], conv_b[l].reshape(1, d),
                     xa_q_g[l].reshape(1, XA_HEAD_DIM),
                     w_br_conv[l].astype(BF16), w_br_fox[l].astype(BF16),
                     w_br_xa[l].astype(BF16), w_o[l].astype(BF16), seq)
        x2 = _ffn(x1, norm2_g[l].reshape(1, d), wg, wu, wo_ffn)
    return x2.reshape(batch, seq, d)
```

```python
import functools
import math

import jax
import jax.numpy as jnp
from jax import lax
from jax.experimental import pallas as pl
from jax.experimental.pallas import tpu as pltpu

F32 = jnp.float32
BF16 = jnp.bfloat16

D_MODEL = 1024
N_GROUPS = 10
FOX_HEADS = 8
FOX_HEAD_DIM = 128
XA_HEADS = 4
XA_HEAD_DIM = 256
D_FF = 2816
FF_CHUNK = 256
CONV_WIDTH = 3
EPS = 1e-6
LOG2E = math.log2(math.e)
LANES = 128
HALO_ROWS = 16

G_CB, G_CC, G_CV, G_FQ, G_FK, G_FV, G_XQ, G_GA, G_GB, G_GC = range(N_GROUPS)

TM_IN = 1024
TQ = 512
TK = 512
FOX_HEADS_PER_STEP = 2
TM_BR = 512
TM_FFN = 512
VMEM_LIMIT = 56 * 1024 * 1024


def _rms(x, g):
    ms = jnp.mean(x * x, axis=-1, keepdims=True)
    return x * lax.rsqrt(ms + EPS) * g


def _dot(a, b):
    return jnp.dot(a, b, preferred_element_type=F32)


def _dot_nt(a, b):
    return lax.dot_general(a, b, (((1,), (1,)), ((), ())), preferred_element_type=F32)


def _const_spec(shape):
    nd = len(shape)
    return pl.BlockSpec(shape, lambda *_: (0,) * nd, pipeline_mode=pl.Buffered(1))


def _mem_kv_kernel(mem_ref, g_ref, w_ref, kg_ref, k_ref, v_ref):
    mn = _rms(mem_ref[0], g_ref[...]).astype(BF16)
    kv = _dot(mn, w_ref[...])
    for h in range(XA_HEADS):
        hs = slice(h * XA_HEAD_DIM, (h + 1) * XA_HEAD_DIM)
        k_ref[0, :, hs] = _rms(kv[:, hs], kg_ref[...]).astype(BF16)
    v_ref[0] = kv[:, D_MODEL:].astype(BF16)


def _mem_kv(mem, g, w, kg):
    b, m, d = mem.shape
    return pl.pallas_call(
        _mem_kv_kernel,
        grid=(b,),
        in_specs=[
            pl.BlockSpec((1, m, d), lambda i: (i, 0, 0)),
            _const_spec((1, d)),
            _const_spec((d, 2 * d)),
            _const_spec((1, XA_HEAD_DIM)),
        ],
        out_specs=[
            pl.BlockSpec((1, m, d), lambda i: (i, 0, 0)),
            pl.BlockSpec((1, m, d), lambda i: (i, 0, 0)),
        ],
        out_shape=[jax.ShapeDtypeStruct((b, m, d), BF16)] * 2,
        compiler_params=pltpu.CompilerParams(
            dimension_semantics=("arbitrary",), vmem_limit_bytes=VMEM_LIMIT),
        name="mem_kv",
    )(mem, g, w, kg)


def _log_sigmoid(x):
    return jnp.minimum(x, 0.0) - jnp.log1p(jnp.exp(-jnp.abs(x)))


def _cumsum_rows(x):
    n = x.shape[0]
    row = lax.broadcasted_iota(jnp.int32, x.shape, 0)
    s = 1
    while s < n:
        x = x + jnp.where(row >= s, pltpu.roll(x, s, 0), 0.0)
        s *= 2
    return x


def _in_proj_kernel(tiles_per_seq, x_ref, g1_ref, w_ref, wff_ref, fb_ref, qg_ref, kg_ref,
                    z_ref, ctok_ref, ct_ref, h_ref, carry_ref):
    i = pl.program_id(0)
    j = pl.program_id(1)
    tm = x_ref.shape[0]

    @pl.when(j == 0)
    def _():
        hb = _rms(x_ref[...], g1_ref[...]).astype(BF16)
        h_ref[...] = hb
        lf = _log_sigmoid(_dot(hb, wff_ref[...]) + fb_ref[...])

        @pl.when(i % tiles_per_seq == 0)
        def _():
            carry_ref[...] = jnp.zeros_like(carry_ref)

        c = _cumsum_rows(lf) + carry_ref[...]
        carry_ref[...] = c[tm - 1:tm, :]
        c = c * LOG2E
        ctok_ref[...] = c
        ct = c.T
        for r in range(tm // TK):
            ct_ref[r] = ct[:FOX_HEADS, r * TK:(r + 1) * TK]

    @pl.when((j != G_FQ) & (j != G_FK))
    def _():
        z_ref[...] = _dot(h_ref[...], w_ref[...]).astype(BF16)

    def head_norm(g_ref, scale):
        r = _dot(h_ref[...], w_ref[...])
        for h in range(FOX_HEADS):
            hs = slice(h * FOX_HEAD_DIM, (h + 1) * FOX_HEAD_DIM)
            z_ref[:, hs] = (_rms(r[:, hs], g_ref[...]) * scale).astype(BF16)

    @pl.when(j == G_FQ)
    def _():
        head_norm(qg_ref, LOG2E / math.sqrt(FOX_HEAD_DIM))

    @pl.when(j == G_FK)
    def _():
        head_norm(kg_ref, 1.0)


def _in_proj(x2, g1, w, wff, fb, qg, kg, seq):
    n, d = x2.shape
    tm = TM_IN
    nt = n // tm
    kern = functools.partial(_in_proj_kernel, seq // tm)
    return pl.pallas_call(
        kern,
        grid=(nt, N_GROUPS),
        in_specs=[
            pl.BlockSpec((tm, d), lambda i, j: (i, 0)),
            _const_spec((1, d)),
            pl.BlockSpec((d, d), lambda i, j: (0, j)),
            _const_spec((d, LANES)),
            _const_spec((1, LANES)),
            _const_spec((1, FOX_HEAD_DIM)),
            _const_spec((1, FOX_HEAD_DIM)),
        ],
        out_specs=[
            pl.BlockSpec((tm, d), lambda i, j: (i, j)),
            pl.BlockSpec((tm, LANES), lambda i, j: (i, 0)),
            pl.BlockSpec((tm // TK, FOX_HEADS, TK), lambda i, j: (i, 0, 0)),
        ],
        out_shape=[
            jax.ShapeDtypeStruct((n, N_GROUPS * d), BF16),
            jax.ShapeDtypeStruct((n, LANES), F32),
            jax.ShapeDtypeStruct((n // TK, FOX_HEADS, TK), F32),
        ],
        scratch_shapes=[
            pltpu.VMEM((tm, d), BF16),
            pltpu.VMEM((1, LANES), F32),
        ],
        compiler_params=pltpu.CompilerParams(
            dimension_semantics=("arbitrary", "arbitrary"), vmem_limit_bytes=VMEM_LIMIT),
        name="in_proj",
    )(x2, g1, w, wff, fb, qg, kg)


def _fox_kernel(q_ref, k_ref, v_ref, cq_ref, ct_ref, o_ref, m_ref, acc_ref):
    qi = pl.program_id(1)
    row = lax.broadcasted_iota(jnp.int32, (TQ, TK), 0)
    col = lax.broadcasted_iota(jnp.int32, (TQ, TK), 1)
    causal = col <= row
    ones = jnp.ones((TK, FOX_HEAD_DIM), BF16)

    def head_step(h, s, cq, kb, masked):
        hs = slice(h * FOX_HEAD_DIM, (h + 1) * FOX_HEAD_DIM)
        off = pl.multiple_of(kb * TK, TK)
        k = k_ref[pl.ds(off, TK), hs]
        v_ext = jnp.concatenate([v_ref[pl.ds(off, TK), hs], ones], axis=1)
        ck = ct_ref[kb, h:h + 1, :]
        t = _dot_nt(q_ref[:, hs], k) - ck
        if masked:
            t = jnp.where(causal, t, -jnp.inf)
        m_old = m_ref[s]
        m_new = jnp.maximum(m_old, jnp.max(t, axis=-1, keepdims=True) + cq)
        p = jnp.exp2(t - jnp.tile(m_new - cq, (1, TK // LANES)))
        alpha = jnp.exp2(m_old - m_new)
        acc_ref[s] = jnp.tile(alpha, (1, 2)) * acc_ref[s] + _dot(p.astype(BF16), v_ext)
        m_ref[s] = m_new

    for h0 in range(0, FOX_HEADS, FOX_HEADS_PER_STEP):
        heads = range(h0, h0 + FOX_HEADS_PER_STEP)
        cqs = [jnp.broadcast_to(cq_ref[:, h:h + 1], (TQ, LANES)) for h in heads]
        m_ref[...] = jnp.full_like(m_ref, -jnp.inf)
        acc_ref[...] = jnp.zeros_like(acc_ref)

        def step(kb, masked):
            for s, h in enumerate(heads):
                head_step(h, s, cqs[s], kb, masked)

        def body(kb, _):
            step(kb, False)
            return 0

        lax.fori_loop(0, qi, body, 0)
        step(qi, True)
        for s, h in enumerate(heads):
            hs = slice(h * FOX_HEAD_DIM, (h + 1) * FOX_HEAD_DIM)
            o_ref[:, hs] = (acc_ref[s, :, :FOX_HEAD_DIM] / acc_ref[s, :, FOX_HEAD_DIM:]).astype(BF16)


def _fox(z, ctok, ct, batch, seq):
    n = z.shape[0]
    d = D_MODEL
    nq = seq // TQ
    nk = seq // TK
    return pl.pallas_call(
        _fox_kernel,
        grid=(batch, nq),
        in_specs=[
            pl.BlockSpec((TQ, d), lambda b, q: (b * nq + q, G_FQ)),
            pl.BlockSpec((seq, d), lambda b, q: (b, G_FK)),
            pl.BlockSpec((seq, d), lambda b, q: (b, G_FV)),
            pl.BlockSpec((TQ, LANES), lambda b, q: (b * nq + q, 0)),
            pl.BlockSpec((nk, FOX_HEADS, TK), lambda b, q: (b, 0, 0)),
        ],
        out_specs=pl.BlockSpec((TQ, d), lambda b, q: (b * nq + q, 0)),
        out_shape=jax.ShapeDtypeStruct((n, d), BF16),
        scratch_shapes=[
            pltpu.VMEM((FOX_HEADS_PER_STEP, TQ, LANES), F32),
            pltpu.VMEM((FOX_HEADS_PER_STEP, TQ, 2 * FOX_HEAD_DIM), F32),
        ],
        compiler_params=pltpu.CompilerParams(
            dimension_semantics=("arbitrary", "arbitrary"), vmem_limit_bytes=VMEM_LIMIT),
        name="fox",
    )(z, z, z, ctok, ct)


def _branch_kernel(tiles_per_seq, cb_ref, cc_ref, cv_ref, ccp_ref, cvp_ref, xq_ref,
                   ga_ref, gb_ref, gc_ref, yf_ref, x_ref, mk_ref, mv_ref,
                   cw_ref, cbias_ref, xqg_ref, wa_ref, wb_ref, wc_ref, wo_ref, o_ref):
    i = pl.program_id(0)
    tm = x_ref.shape[0]

    u = cc_ref[...].astype(F32) * cv_ref[...].astype(F32)
    first = (i % tiles_per_seq) == 0
    prev = ccp_ref[...].astype(F32) * cvp_ref[...].astype(F32)
    prev = jnp.where(first, 0.0, prev)
    hrow = lax.broadcasted_iota(jnp.int32, (HALO_ROWS, D_MODEL), 0)
    u1 = pltpu.roll(u, 1, 0)
    u2 = pltpu.roll(u, 2, 0)
    u1 = jnp.concatenate(
        [jnp.where(hrow < 1, pltpu.roll(prev, 1, 0), u1[:HALO_ROWS]), u1[HALO_ROWS:]], axis=0)
    u2 = jnp.concatenate(
        [jnp.where(hrow < 2, pltpu.roll(prev, 2, 0), u2[:HALO_ROWS]), u2[HALO_ROWS:]], axis=0)
    conv = cw_ref[0:1, :] * u2 + cw_ref[1:2, :] * u1 + cw_ref[2:3, :] * u
    y_conv = cb_ref[...].astype(F32) * (conv + cbias_ref[...])
    merged = jax.nn.sigmoid(ga_ref[...].astype(F32)) * _dot(y_conv.astype(BF16), wa_ref[...])

    merged += jax.nn.sigmoid(gb_ref[...].astype(F32)) * _dot(yf_ref[...], wb_ref[...])

    heads = []
    for h in range(XA_HEADS):
        hs = slice(h * XA_HEAD_DIM, (h + 1) * XA_HEAD_DIM)
        q = _rms(xq_ref[:, hs].astype(F32), xqg_ref[...]) * (1.0 / math.sqrt(XA_HEAD_DIM))
        s = _dot_nt(q.astype(BF16), mk_ref[0, :, hs])
        p = jnp.exp(s - jnp.max(s, axis=-1, keepdims=True))
        o = _dot(p.astype(BF16), mv_ref[0, :, hs])
        heads.append(o / jnp.sum(p, axis=-1, keepdims=True))
    y_xa = jnp.concatenate(heads, axis=-1).astype(BF16)
    merged += jax.nn.sigmoid(gc_ref[...].astype(F32)) * _dot(y_xa, wc_ref[...])

    o_ref[...] = x_ref[...] + _dot(merged.astype(BF16), wo_ref[...])


def _branch(z, y_fox, x2, mk, mv, cw, cbias, xqg, wa, wb, wc, wo, seq):
    n, d = x2.shape
    tm = TM_BR
    nt = n // tm
    tps = seq // tm
    n_mem = mk.shape[1]
    hpt = tm // HALO_ROWS

    def zspec(g):
        return pl.BlockSpec((tm, d), lambda i: (i, g))

    def halo(g):
        return pl.BlockSpec((HALO_ROWS, d), lambda i: (jnp.maximum(i * hpt - 1, 0), g))

    kern = functools.partial(_branch_kernel, tps)
    return pl.pallas_call(
        kern,
        grid=(nt,),
        in_specs=[
            zspec(G_CB), zspec(G_CC), zspec(G_CV), halo(G_CC), halo(G_CV), zspec(G_XQ),
            zspec(G_GA), zspec(G_GB), zspec(G_GC),
            pl.BlockSpec((tm, d), lambda i: (i, 0)),
            pl.BlockSpec((tm, d), lambda i: (i, 0)),
            pl.BlockSpec((1, n_mem, d), lambda i: (i // tps, 0, 0)),
            pl.BlockSpec((1, n_mem, d), lambda i: (i // tps, 0, 0)),
            _const_spec((CONV_WIDTH, d)),
            _const_spec((1, d)),
            _const_spec((1, XA_HEAD_DIM)),
            _const_spec((d, d)), _const_spec((d, d)), _const_spec((d, d)), _const_spec((d, d)),
        ],
        out_specs=pl.BlockSpec((tm, d), lambda i: (i, 0)),
        out_shape=jax.ShapeDtypeStruct((n, d), F32),
        compiler_params=pltpu.CompilerParams(
            dimension_semantics=("arbitrary",), vmem_limit_bytes=VMEM_LIMIT),
        name="branch",
    )(z, z, z, z, z, z, z, z, z, y_fox, x2, mk, mv, cw, cbias, xqg, wa, wb, wc, wo)


def _ffn_kernel(x_ref, g_ref, wg_ref, wu_ref, wo_ref, o_ref, h_ref, acc_ref):
    x = x_ref[...]
    h_ref[...] = _rms(x, g_ref[...]).astype(BF16)
    acc_ref[...] = x

    def body(c, _):
        h = h_ref[...]
        g = _dot(h, wg_ref[c])
        u = _dot(h, wu_ref[c])
        a = (g * jax.nn.sigmoid(g) * u).astype(BF16)
        acc_ref[...] += _dot(a, wo_ref[c])
        return 0

    lax.fori_loop(0, wg_ref.shape[0], body, 0)
    o_ref[...] = acc_ref[...]


def _ffn(x1, g, wg, wu, wo):
    n, d = x1.shape
    tm = TM_FFN
    nc = wg.shape[0]
    return pl.pallas_call(
        _ffn_kernel,
        grid=(n // tm,),
        in_specs=[
            pl.BlockSpec((tm, d), lambda i: (i, 0)),
            _const_spec((1, d)),
            _const_spec((nc, d, FF_CHUNK)),
            _const_spec((nc, d, FF_CHUNK)),
            _const_spec((nc, FF_CHUNK, d)),
        ],
        out_specs=pl.BlockSpec((tm, d), lambda i: (i, 0)),
        out_shape=jax.ShapeDtypeStruct((n, d), F32),
        scratch_shapes=[
            pltpu.VMEM((tm, d), BF16),
            pltpu.VMEM((tm, d), F32),
        ],
        compiler_params=pltpu.CompilerParams(
            dimension_semantics=("arbitrary",), vmem_limit_bytes=VMEM_LIMIT),
        name="ffn",
    )(x1, g, wg, wu, wo)


def kernel(x, mem, norm1_g, w_in, conv_w, conv_b, fox_f_bias, fox_q_g, fox_k_g, mem_norm_g,
           w_mem_kv, xa_q_g, xa_k_g, w_br_conv, w_br_fox, w_br_xa, w_o, norm2_g, w_ffn_in,
           w_ffn_out):
    batch, seq, d = x.shape
    depth = norm1_g.shape[0]
    n_main = N_GROUPS * d
    nc = D_FF // FF_CHUNK
    x2 = x.reshape(batch * seq, d)
    for l in range(depth):
        w_main = w_in[l, :, :n_main].astype(BF16)
        w_ff = jnp.pad(w_in[l, :, n_main:], ((0, 0), (0, LANES - FOX_HEADS))).astype(BF16)
        f_bias = jnp.pad(fox_f_bias[l], (0, LANES - FOX_HEADS)).reshape(1, LANES)
        wg = w_ffn_in[l, :, :D_FF].astype(BF16).reshape(d, nc, FF_CHUNK).transpose(1, 0, 2)
        wu = w_ffn_in[l, :, D_FF:].astype(BF16).reshape(d, nc, FF_CHUNK).transpose(1, 0, 2)
        wo_ffn = w_ffn_out[l].astype(BF16).reshape(nc, FF_CHUNK, d)

        mk, mv = _mem_kv(mem, mem_norm_g[l].reshape(1, d), w_mem_kv[l].astype(BF16),
                         xa_k_g[l].reshape(1, XA_HEAD_DIM))
        z, ctok, ct = _in_proj(x2, norm1_g[l].reshape(1, d), w_main, w_ff, f_bias,
                               fox_q_g[l].reshape(1, FOX_HEAD_DIM),
                               fox_k_g[l].reshape(1, FOX_HEAD_DIM), seq)
        y_fox = _fox(z, ctok, ct, batch, seq)
        x1 = _branch(z, y_fox, x2, mk, mv, conv_w[l], conv_b[l].reshape(1, d),
                     xa_q_g[l].reshape(1, XA_HEAD_DIM),
                     w_br_conv[l].astype(BF16), w_br_fox[l].astype(BF16),
                     w_br_xa[l].astype(BF16), w_o[l].astype(BF16), seq)
        x2 = _ffn(x1, norm2_g[l].reshape(1, d), wg, wu, wo_ffn)
    return x2.reshape(batch, seq, d)
```

```python
import functools
import math

import jax
import jax.numpy as jnp
from jax import lax
from jax.experimental import pallas as pl
from jax.experimental.pallas import tpu as pltpu

F32 = jnp.float32
BF16 = jnp.bfloat16

D_MODEL = 1024
N_GROUPS = 10
FOX_HEADS = 8
FOX_HEAD_DIM = 128
XA_HEADS = 4
XA_HEAD_DIM = 256
D_FF = 2816
FF_CHUNK = 256
CONV_WIDTH = 3
EPS = 1e-6
LOG2E = math.log2(math.e)
LANES = 128
HALO_ROWS = 16

G_CB, G_CC, G_CV, G_FQ, G_FK, G_FV, G_XQ, G_GA, G_GB, G_GC = range(N_GROUPS)

TM_IN = 1024
TQ = 512
TK = 512
FOX_HEADS_PER_STEP = 2
TM_BR = 512
TM_FFN = 512
VMEM_LIMIT = 56 * 1024 * 1024


def _rms(x, g):
    ms = jnp.mean(x * x, axis=-1, keepdims=True)
    return x * lax.rsqrt(ms + EPS) * g


def _dot(a, b):
    return jnp.dot(a, b, preferred_element_type=F32)


def _dot_nt(a, b):
    return lax.dot_general(a, b, (((1,), (1,)), ((), ())), preferred_element_type=F32)


def _const_spec(shape):
    nd = len(shape)
    return pl.BlockSpec(shape, lambda *_: (0,) * nd, pipeline_mode=pl.Buffered(1))


def _mem_kv_kernel(mem_ref, g_ref, w_ref, kg_ref, k_ref, v_ref):
    mn = _rms(mem_ref[0], g_ref[...]).astype(BF16)
    kv = _dot(mn, w_ref[...])
    for h in range(XA_HEADS):
        hs = slice(h * XA_HEAD_DIM, (h + 1) * XA_HEAD_DIM)
        k_ref[0, :, hs] = _rms(kv[:, hs], kg_ref[...]).astype(BF16)
    v_ref[0] = kv[:, D_MODEL:].astype(BF16)


def _mem_kv(mem, g, w, kg):
    b, m, d = mem.shape
    return pl.pallas_call(
        _mem_kv_kernel,
        grid=(b,),
        in_specs=[
            pl.BlockSpec((1, m, d), lambda i: (i, 0, 0)),
            _const_spec((1, d)),
            _const_spec((d, 2 * d)),
            _const_spec((1, XA_HEAD_DIM)),
        ],
        out_specs=[
            pl.BlockSpec((1, m, d), lambda i: (i, 0, 0)),
            pl.BlockSpec((1, m, d), lambda i: (i, 0, 0)),
        ],
        out_shape=[jax.ShapeDtypeStruct((b, m, d), BF16)] * 2,
        compiler_params=pltpu.CompilerParams(
            dimension_semantics=("arbitrary",), vmem_limit_bytes=VMEM_LIMIT),
        name="mem_kv",
    )(mem, g, w, kg)


def _log_sigmoid(x):
    return jnp.minimum(x, 0.0) - jnp.log1p(jnp.exp(-jnp.abs(x)))


def _cumsum_rows(x):
    n = x.shape[0]
    row = lax.broadcasted_iota(jnp.int32, x.shape, 0)
    s = 1
    while s < n:
        x = x + jnp.where(row >= s, pltpu.roll(x, s, 0), 0.0)
        s *= 2
    return x


def _in_proj_kernel(tiles_per_seq, x_ref, g1_ref, w_ref, wff_ref, fb_ref, qg_ref, kg_ref,
                    z_ref, ctok_ref, ct_ref, h_ref, carry_ref):
    i = pl.program_id(0)
    j = pl.program_id(1)
    tm = x_ref.shape[0]

    @pl.when(j == 0)
    def _():
        hb = _rms(x_ref[...], g1_ref[...]).astype(BF16)
        h_ref[...] = hb
        lf = _log_sigmoid(_dot(hb, wff_ref[...]) + fb_ref[...])

        @pl.when(i % tiles_per_seq == 0)
        def _():
            carry_ref[...] = jnp.zeros_like(carry_ref)

        c = _cumsum_rows(lf) + carry_ref[...]
        carry_ref[...] = c[tm - 1:tm, :]
        c = c * LOG2E
        ctok_ref[...] = c
        ct = c.T
        for r in range(tm // TK):
            ct_ref[r] = ct[:FOX_HEADS, r * TK:(r + 1) * TK]

    @pl.when((j != G_FQ) & (j != G_FK))
    def _():
        z_ref[...] = _dot(h_ref[...], w_ref[...]).astype(BF16)

    def head_norm(g_ref, scale):
        r = _dot(h_ref[...], w_ref[...])
        for h in range(FOX_HEADS):
            hs = slice(h * FOX_HEAD_DIM, (h + 1) * FOX_HEAD_DIM)
            z_ref[:, hs] = (_rms(r[:, hs], g_ref[...]) * scale).astype(BF16)

    @pl.when(j == G_FQ)
    def _():
        head_norm(qg_ref, LOG2E / math.sqrt(FOX_HEAD_DIM))

    @pl.when(j == G_FK)
    def _():
        head_norm(kg_ref, 1.0)


def _in_proj(x2, g1, w, wff, fb, qg, kg, seq):
    n, d = x2.shape
    tm = TM_IN
    nt = n // tm
    kern = functools.partial(_in_proj_kernel, seq // tm)
    return pl.pallas_call(
        kern,
        grid=(nt, N_GROUPS),
        in_specs=[
            pl.BlockSpec((tm, d), lambda i, j: (i, 0)),
            _const_spec((1, d)),
            pl.BlockSpec((d, d), lambda i, j: (0, j)),
            _const_spec((d, LANES)),
            _const_spec((1, LANES)),
            _const_spec((1, FOX_HEAD_DIM)),
            _const_spec((1, FOX_HEAD_DIM)),
        ],
        out_specs=[
            pl.BlockSpec((tm, d), lambda i, j: (i, j)),
            pl.BlockSpec((tm, LANES), lambda i, j: (i, 0)),
            pl.BlockSpec((tm // TK, FOX_HEADS, TK), lambda i, j: (i, 0, 0)),
        ],
        out_shape=[
            jax.ShapeDtypeStruct((n, N_GROUPS * d), BF16),
            jax.ShapeDtypeStruct((n, LANES), F32),
            jax.ShapeDtypeStruct((n // TK, FOX_HEADS, TK), F32),
        ],
        scratch_shapes=[
            pltpu.VMEM((tm, d), BF16),
            pltpu.VMEM((1, LANES), F32),
        ],
        compiler_params=pltpu.CompilerParams(
            dimension_semantics=("arbitrary", "arbitrary"), vmem_limit_bytes=VMEM_LIMIT),
        name="in_proj",
    )(x2, g1, w, wff, fb, qg, kg)


def _fox_kernel(q_ref, k_ref, v_ref, cq_ref, ct_ref, o_ref):
    qi = pl.program_id(1)
    row = lax.broadcasted_iota(jnp.int32, (TQ, TK), 0)
    col = lax.broadcasted_iota(jnp.int32, (TQ, TK), 1)
    causal = col <= row
    ones = jnp.ones((TK, FOX_HEAD_DIM), BF16)

    def logits(h, kb):
        hs = slice(h * FOX_HEAD_DIM, (h + 1) * FOX_HEAD_DIM)
        k = k_ref[kb * TK:(kb + 1) * TK, hs]
        return _dot_nt(q_ref[:, hs], k) - ct_ref[kb, h:h + 1, :]

    def accumulate(h, kb, t, cq, m_old, acc_old, masked):
        hs = slice(h * FOX_HEAD_DIM, (h + 1) * FOX_HEAD_DIM)
        if masked:
            t = jnp.where(causal, t, -jnp.inf)
        m_new = jnp.max(t, axis=-1, keepdims=True) + cq
        if m_old is not None:
            m_new = jnp.maximum(m_old, m_new)
        p = jnp.exp2(t - jnp.tile(m_new - cq, (1, TK // LANES)))
        v_ext = jnp.concatenate([v_ref[kb * TK:(kb + 1) * TK, hs], ones], axis=1)
        acc = _dot(p.astype(BF16), v_ext)
        if acc_old is not None:
            acc += jnp.tile(jnp.exp2(m_old - m_new), (1, 2)) * acc_old
        return m_new, acc

    def q_tile(n):
        for h0 in range(0, FOX_HEADS, FOX_HEADS_PER_STEP):
            heads = range(h0, h0 + FOX_HEADS_PER_STEP)
            cq = [jnp.broadcast_to(cq_ref[:, h:h + 1], (TQ, LANES)) for h in heads]
            m = [None] * FOX_HEADS_PER_STEP
            acc = [None] * FOX_HEADS_PER_STEP
            t = [logits(h, 0) for h in heads]
            for kb in range(n + 1):
                if kb < n:
                    t_next = [logits(h, kb + 1) for h in heads]
                for s, h in enumerate(heads):
                    m[s], acc[s] = accumulate(h, kb, t[s], cq[s], m[s], acc[s], kb == n)
                if kb < n:
                    t = t_next
            for s, h in enumerate(heads):
                hs = slice(h * FOX_HEAD_DIM, (h + 1) * FOX_HEAD_DIM)
                o_ref[:, hs] = (acc[s][:, :FOX_HEAD_DIM] / acc[s][:, FOX_HEAD_DIM:]).astype(BF16)

    for n in range(k_ref.shape[0] // TQ):
        pl.when(qi == n)(functools.partial(q_tile, n))


def _fox(z, ctok, ct, batch, seq):
    n = z.shape[0]
    d = D_MODEL
    nq = seq // TQ
    nk = seq // TK
    return pl.pallas_call(
        _fox_kernel,
        grid=(batch, nq),
        in_specs=[
            pl.BlockSpec((TQ, d), lambda b, q: (b * nq + q, G_FQ)),
            pl.BlockSpec((seq, d), lambda b, q: (b, G_FK)),
            pl.BlockSpec((seq, d), lambda b, q: (b, G_FV)),
            pl.BlockSpec((TQ, LANES), lambda b, q: (b * nq + q, 0)),
            pl.BlockSpec((nk, FOX_HEADS, TK), lambda b, q: (b, 0, 0)),
        ],
        out_specs=pl.BlockSpec((TQ, d), lambda b, q: (b * nq + q, 0)),
        out_shape=jax.ShapeDtypeStruct((n, d), BF16),
        compiler_params=pltpu.CompilerParams(
            dimension_semantics=("arbitrary", "arbitrary"), vmem_limit_bytes=VMEM_LIMIT),
        name="fox",
    )(z, z, z, ctok, ct)


def _branch_kernel(tiles_per_seq, cb_ref, cc_ref, cv_ref, ccp_ref, cvp_ref, xq_ref,
                   ga_ref, gb_ref, gc_ref, yf_ref, x_ref, mk_ref, mv_ref,
                   cw_ref, cbias_ref, xqg_ref, wa_ref, wb_ref, wc_ref, wo_ref, o_ref):
    i = pl.program_id(0)
    tm = x_ref.shape[0]

    u = cc_ref[...].astype(F32) * cv_ref[...].astype(F32)
    first = (i % tiles_per_seq) == 0
    prev = ccp_ref[...].astype(F32) * cvp_ref[...].astype(F32)
    prev = jnp.where(first, 0.0, prev)
    hrow = lax.broadcasted_iota(jnp.int32, (HALO_ROWS, D_MODEL), 0)
    u1 = pltpu.roll(u, 1, 0)
    u2 = pltpu.roll(u, 2, 0)
    u1 = jnp.concatenate(
        [jnp.where(hrow < 1, pltpu.roll(prev, 1, 0), u1[:HALO_ROWS]), u1[HALO_ROWS:]], axis=0)
    u2 = jnp.concatenate(
        [jnp.where(hrow < 2, pltpu.roll(prev, 2, 0), u2[:HALO_ROWS]), u2[HALO_ROWS:]], axis=0)
    conv = cw_ref[0:1, :] * u2 + cw_ref[1:2, :] * u1 + cw_ref[2:3, :] * u
    y_conv = cb_ref[...].astype(F32) * (conv + cbias_ref[...])
    merged = jax.nn.sigmoid(ga_ref[...].astype(F32)) * _dot(y_conv.astype(BF16), wa_ref[...])

    merged += jax.nn.sigmoid(gb_ref[...].astype(F32)) * _dot(yf_ref[...], wb_ref[...])

    heads = []
    for h in range(XA_HEADS):
        hs = slice(h * XA_HEAD_DIM, (h + 1) * XA_HEAD_DIM)
        q = _rms(xq_ref[:, hs].astype(F32), xqg_ref[...]) * (1.0 / math.sqrt(XA_HEAD_DIM))
        s = _dot_nt(q.astype(BF16), mk_ref[0, :, hs])
        p = jnp.exp(s - jnp.max(s, axis=-1, keepdims=True))
        o = _dot(p.astype(BF16), mv_ref[0, :, hs])
        heads.append(o / jnp.sum(p, axis=-1, keepdims=True))
    y_xa = jnp.concatenate(heads, axis=-1).astype(BF16)
    merged += jax.nn.sigmoid(gc_ref[...].astype(F32)) * _dot(y_xa, wc_ref[...])

    o_ref[...] = x_ref[...] + _dot(merged.astype(BF16), wo_ref[...])


def _branch(z, y_fox, x2, mk, mv, cw, cbias, xqg, wa, wb, wc, wo, seq):
    n, d = x2.shape
    tm = TM_BR
    nt = n // tm
    tps = seq // tm
    n_mem = mk.shape[1]
    hpt = tm // HALO_ROWS

    def zspec(g):
        return pl.BlockSpec((tm, d), lambda i: (i, g))

    def halo(g):
        return pl.BlockSpec((HALO_ROWS, d), lambda i: (jnp.maximum(i * hpt - 1, 0), g))

    kern = functools.partial(_branch_kernel, tps)
    return pl.pallas_call(
        kern,
        grid=(nt,),
        in_specs=[
            zspec(G_CB), zspec(G_CC), zspec(G_CV), halo(G_CC), halo(G_CV), zspec(G_XQ),
            zspec(G_GA), zspec(G_GB), zspec(G_GC),
            pl.BlockSpec((tm, d), lambda i: (i, 0)),
            pl.BlockSpec((tm, d), lambda i: (i, 0)),
            pl.BlockSpec((1, n_mem, d), lambda i: (i // tps, 0, 0)),
            pl.BlockSpec((1, n_mem, d), lambda i: (i // tps, 0, 0)),
            _const_spec((CONV_WIDTH, d)),
            _const_spec((1, d)),
            _const_spec((1, XA_HEAD_DIM)),
            _const_spec((d, d)), _const_spec((d, d)), _const_spec((d, d)), _const_spec((d, d)),
        ],
        out_specs=pl.BlockSpec((tm, d), lambda i: (i, 0)),
        out_shape=jax.ShapeDtypeStruct((n, d), F32),
        compiler_params=pltpu.CompilerParams(
            dimension_semantics=("arbitrary",), vmem_limit_bytes=VMEM_LIMIT),
        name="branch",
    )(z, z, z, z, z, z, z, z, z, y_fox, x2, mk, mv, cw, cbias, xqg, wa, wb, wc, wo)


def _ffn_kernel(x_ref, g_ref, wi_ref, wo_ref, o_ref, a_ref):
    x = x_ref[...]
    h = _rms(x, g_ref[...]).astype(BF16)
    for c in range(D_FF // FF_CHUNK):
        g = _dot(h, wi_ref[:, c * FF_CHUNK:(c + 1) * FF_CHUNK])
        u = _dot(h, wi_ref[:, D_FF + c * FF_CHUNK:D_FF + (c + 1) * FF_CHUNK])
        a_ref[:, c * FF_CHUNK:(c + 1) * FF_CHUNK] = (g * jax.nn.sigmoid(g) * u).astype(BF16)
    o_ref[...] = x + _dot(a_ref[...], wo_ref[...])


def _ffn(x1, g, wi, wo):
    n, d = x1.shape
    tm = TM_FFN
    return pl.pallas_call(
        _ffn_kernel,
        grid=(n // tm,),
        in_specs=[
            pl.BlockSpec((tm, d), lambda i: (i, 0)),
            _const_spec((1, d)),
            _const_spec((d, 2 * D_FF)),
            _const_spec((D_FF, d)),
        ],
        out_specs=pl.BlockSpec((tm, d), lambda i: (i, 0)),
        out_shape=jax.ShapeDtypeStruct((n, d), F32),
        scratch_shapes=[pltpu.VMEM((tm, D_FF), BF16)],
        compiler_params=pltpu.CompilerParams(
            dimension_semantics=("arbitrary",), vmem_limit_bytes=VMEM_LIMIT),
        name="ffn",
    )(x1, g, wi, wo)


def kernel(x, mem, norm1_g, w_in, conv_w, conv_b, fox_f_bias, fox_q_g, fox_k_g, mem_norm_g,
           w_mem_kv, xa_q_g, xa_k_g, w_br_conv, w_br_fox, w_br_xa, w_o, norm2_g, w_ffn_in,
           w_ffn_out):
    batch, seq, d = x.shape
    depth = norm1_g.shape[0]
    n_main = N_GROUPS * d
    x2 = x.reshape(batch * seq, d)
    for l in range(depth):
        w_main = w_in[l, :, :n_main].astype(BF16)
        w_ff = jnp.pad(w_in[l, :, n_main:], ((0, 0), (0, LANES - FOX_HEADS))).astype(BF16)
        f_bias = jnp.pad(fox_f_bias[l], (0, LANES - FOX_HEADS)).reshape(1, LANES)

        mk, mv = _mem_kv(mem, mem_norm_g[l].reshape(1, d), w_mem_kv[l].astype(BF16),
                         xa_k_g[l].reshape(1, XA_HEAD_DIM))
        z, ctok, ct = _in_proj(x2, norm1_g[l].reshape(1, d), w_main, w_ff, f_bias,
                               fox_q_g[l].reshape(1, FOX_HEAD_DIM),
                               fox_k_g[l].reshape(1, FOX_HEAD_DIM), seq)
        y_fox = _fox(z, ctok, ct, batch, seq)
        x1 = _branch(z, y_fox, x2, mk, mv, conv_w[l], conv_b[l].reshape(1, d),
                     xa_q_g[l].reshape(1, XA_HEAD_DIM),
                     w_br_conv[l].astype(BF16), w_br_fox[l].astype(BF16),
                     w_br_xa[l].astype(BF16), w_o[l].astype(BF16), seq)
        x2 = _ffn(x1, norm2_g[l].reshape(1, d), w_ffn_in[l].astype(BF16),
                  w_ffn_out[l].astype(BF16))
    return x2.reshape(batch, seq, d)
```

```python
import functools
import math

import jax
import jax.numpy as jnp
from jax import lax
from jax.experimental import pallas as pl
from jax.experimental.pallas import tpu as pltpu

F32 = jnp.float32
BF16 = jnp.bfloat16

D_MODEL = 1024
N_GROUPS = 10
FOX_HEADS = 8
FOX_HEAD_DIM = 128
XA_HEADS = 4
XA_HEAD_DIM = 256
D_FF = 2816
FF_CHUNK = 256
CONV_WIDTH = 3
EPS = 1e-6
LOG2E = math.log2(math.e)
LANES = 128
HALO_ROWS = 16

G_CB, G_CC, G_CV, G_FQ, G_FK, G_FV, G_XQ, G_GA, G_GB, G_GC = range(N_GROUPS)

TM_IN = 512
TQ = 512
TK = 512
FOX_HEADS_PER_STEP = 2
FOX_ROW_CHUNK = 64
TM_BR = 512
TM_FFN = 512
VMEM_LIMIT = 56 * 1024 * 1024
VMEM_LIMIT_IN_PROJ = 60 * 1024 * 1024


def _rms(x, g):
    ms = jnp.mean(x * x, axis=-1, keepdims=True)
    return x * lax.rsqrt(ms + EPS) * g


def _dot(a, b):
    return jnp.dot(a, b, preferred_element_type=F32)


def _dot_nt(a, b):
    return lax.dot_general(a, b, (((1,), (1,)), ((), ())), preferred_element_type=F32)


def _const_spec(shape):
    nd = len(shape)
    return pl.BlockSpec(shape, lambda *_: (0,) * nd, pipeline_mode=pl.Buffered(1))


def _mem_kv_kernel(mem_ref, g_ref, w_ref, kg_ref, k_ref, v_ref):
    mn = _rms(mem_ref[0], g_ref[...]).astype(BF16)
    kv = _dot(mn, w_ref[...])
    for h in range(XA_HEADS):
        hs = slice(h * XA_HEAD_DIM, (h + 1) * XA_HEAD_DIM)
        k_ref[0, :, hs] = _rms(kv[:, hs], kg_ref[...]).astype(BF16)
    v_ref[0] = kv[:, D_MODEL:].astype(BF16)


def _mem_kv(mem, g, w, kg):
    b, m, d = mem.shape
    return pl.pallas_call(
        _mem_kv_kernel,
        grid=(b,),
        in_specs=[
            pl.BlockSpec((1, m, d), lambda i: (i, 0, 0)),
            _const_spec((1, d)),
            _const_spec((d, 2 * d)),
            _const_spec((1, XA_HEAD_DIM)),
        ],
        out_specs=[
            pl.BlockSpec((1, m, d), lambda i: (i, 0, 0)),
            pl.BlockSpec((1, m, d), lambda i: (i, 0, 0)),
        ],
        out_shape=[jax.ShapeDtypeStruct((b, m, d), BF16)] * 2,
        compiler_params=pltpu.CompilerParams(
            dimension_semantics=("arbitrary",), vmem_limit_bytes=VMEM_LIMIT),
        name="mem_kv",
    )(mem, g, w, kg)


def _log_sigmoid(x):
    return jnp.minimum(x, 0.0) - jnp.log1p(jnp.exp(-jnp.abs(x)))


def _cumsum_rows(x):
    n = x.shape[0]
    row = lax.broadcasted_iota(jnp.int32, x.shape, 0)
    s = 1
    while s < n:
        x = x + jnp.where(row >= s, pltpu.roll(x, s, 0), 0.0)
        s *= 2
    return x


def _in_proj_kernel(tiles_per_seq, x_ref, g1_ref, w_ref, wff_ref, fb_ref, qg_ref, kg_ref,
                    xqg_ref, z_ref, ctok_ref, ct_ref, carry_ref):
    i = pl.program_id(0)
    tm = x_ref.shape[0]
    d = x_ref.shape[1]
    hb = _rms(x_ref[...], g1_ref[...]).astype(BF16)

    lf = _log_sigmoid(_dot(hb, wff_ref[...]) + fb_ref[...])

    @pl.when(i % tiles_per_seq == 0)
    def _():
        carry_ref[...] = jnp.zeros_like(carry_ref)

    c = _cumsum_rows(lf) + carry_ref[...]
    carry_ref[...] = c[tm - 1:tm, :]
    c = c * LOG2E
    ctok_ref[...] = c
    ct = c.T
    for r in range(tm // TK):
        ct_ref[r] = ct[:FOX_HEADS, r * TK:(r + 1) * TK]

    def head_norm(r, g_ref, scale, col0):
        hd = g_ref.shape[1]
        for h in range(d // hd):
            z_ref[:, col0 + h * hd:col0 + (h + 1) * hd] = (
                _rms(r[:, h * hd:(h + 1) * hd], g_ref[...]) * scale).astype(BF16)

    for g in range(N_GROUPS):
        r = _dot(hb, w_ref[:, g * d:(g + 1) * d])
        if g == G_FQ:
            head_norm(r, qg_ref, LOG2E / math.sqrt(FOX_HEAD_DIM), g * d)
        elif g == G_FK:
            head_norm(r, kg_ref, 1.0, g * d)
        elif g == G_XQ:
            head_norm(r, xqg_ref, LOG2E / math.sqrt(XA_HEAD_DIM), g * d)
        elif g in (G_GA, G_GB, G_GC):
            z_ref[:, g * d:(g + 1) * d] = jax.nn.sigmoid(r).astype(BF16)
        else:
            z_ref[:, g * d:(g + 1) * d] = r.astype(BF16)


def _in_proj(x2, g1, w, wff, fb, qg, kg, xqg, seq):
    n, d = x2.shape
    tm = TM_IN
    kern = functools.partial(_in_proj_kernel, seq // tm)
    return pl.pallas_call(
        kern,
        grid=(n // tm,),
        in_specs=[
            pl.BlockSpec((tm, d), lambda i: (i, 0)),
            _const_spec((1, d)),
            _const_spec(w.shape),
            _const_spec((d, LANES)),
            _const_spec((1, LANES)),
            _const_spec((1, FOX_HEAD_DIM)),
            _const_spec((1, FOX_HEAD_DIM)),
            _const_spec((1, XA_HEAD_DIM)),
        ],
        out_specs=[
            pl.BlockSpec((tm, N_GROUPS * d), lambda i: (i, 0)),
            pl.BlockSpec((tm, LANES), lambda i: (i, 0)),
            pl.BlockSpec((tm // TK, FOX_HEADS, TK), lambda i: (i, 0, 0)),
        ],
        out_shape=[
            jax.ShapeDtypeStruct((n, N_GROUPS * d), BF16),
            jax.ShapeDtypeStruct((n, LANES), F32),
            jax.ShapeDtypeStruct((n // TK, FOX_HEADS, TK), F32),
        ],
        scratch_shapes=[pltpu.VMEM((1, LANES), F32)],
        compiler_params=pltpu.CompilerParams(
            dimension_semantics=("arbitrary",), vmem_limit_bytes=VMEM_LIMIT_IN_PROJ),
        name="in_proj",
    )(x2, g1, w, wff, fb, qg, kg, xqg)


def _fox_kernel(q_ref, k_ref, v_ref, cq_ref, ct_ref, o_ref,
                t_ref, p_ref, cqb_ref, m_ref, alpha_ref, acc_ref):
    qi = pl.program_id(1)
    row = lax.broadcasted_iota(jnp.int32, (TQ, TK), 0)
    col = lax.broadcasted_iota(jnp.int32, (TQ, TK), 1)
    causal = col <= row
    ones = jnp.ones((TK, FOX_HEAD_DIM), BF16)

    def logits(h, s, kb, masked):
        hs = slice(h * FOX_HEAD_DIM, (h + 1) * FOX_HEAD_DIM)
        t = _dot_nt(q_ref[:, hs], k_ref[kb * TK:(kb + 1) * TK, hs]) - ct_ref[kb, h:h + 1, :]
        if masked:
            t = jnp.where(causal, t, -jnp.inf)
        t_ref[kb % 2, s] = t

    def accumulate(h, s, kb):
        hs = slice(h * FOX_HEAD_DIM, (h + 1) * FOX_HEAD_DIM)
        par = kb % 2
        for r in range(0, TQ, FOX_ROW_CHUNK):
            rows = slice(r, r + FOX_ROW_CHUNK)
            t = t_ref[par, s, rows, :]
            cq = cqb_ref[s, rows, :]
            m_new = jnp.max(t, axis=-1, keepdims=True) + cq
            if kb > 0:
                m_old = m_ref[s, rows, :]
                m_new = jnp.maximum(m_old, m_new)
                alpha_ref[s, rows, :] = jnp.exp2(m_old - m_new)
            m_ref[s, rows, :] = m_new
            p_ref[par, s, rows, :] = jnp.exp2(
                t - jnp.tile(m_new - cq, (1, TK // LANES))).astype(BF16)
        v_ext = jnp.concatenate([v_ref[kb * TK:(kb + 1) * TK, hs], ones], axis=1)
        pv = _dot(p_ref[par, s], v_ext)
        if kb > 0:
            pv += jnp.tile(alpha_ref[s], (1, 2)) * acc_ref[s]
        acc_ref[s] = pv

    def q_tile(n):
        for h0 in range(0, FOX_HEADS, FOX_HEADS_PER_STEP):
            heads = list(enumerate(range(h0, h0 + FOX_HEADS_PER_STEP)))
            for s, h in heads:
                cqb_ref[s] = jnp.broadcast_to(cq_ref[:, h:h + 1], (TQ, LANES))
                logits(h, s, 0, n == 0)
            for kb in range(n + 1):
                if kb < n:
                    for s, h in heads:
                        logits(h, s, kb + 1, kb + 1 == n)
                for s, h in heads:
                    accumulate(h, s, kb)
            for s, h in heads:
                hs = slice(h * FOX_HEAD_DIM, (h + 1) * FOX_HEAD_DIM)
                o_ref[:, hs] = (acc_ref[s, :, :FOX_HEAD_DIM]
                                / acc_ref[s, :, FOX_HEAD_DIM:]).astype(BF16)

    for n in range(k_ref.shape[0] // TQ):
        pl.when(qi == n)(functools.partial(q_tile, n))


def _fox(z, ctok, ct, batch, seq):
    n = z.shape[0]
    d = D_MODEL
    nq = seq // TQ
    nk = seq // TK
    return pl.pallas_call(
        _fox_kernel,
        grid=(batch, nq),
        in_specs=[
            pl.BlockSpec((TQ, d), lambda b, q: (b * nq + q, G_FQ)),
            pl.BlockSpec((seq, d), lambda b, q: (b, G_FK)),
            pl.BlockSpec((seq, d), lambda b, q: (b, G_FV)),
            pl.BlockSpec((TQ, LANES), lambda b, q: (b * nq + q, 0)),
            pl.BlockSpec((nk, FOX_HEADS, TK), lambda b, q: (b, 0, 0)),
        ],
        out_specs=pl.BlockSpec((TQ, d), lambda b, q: (b * nq + q, 0)),
        out_shape=jax.ShapeDtypeStruct((n, d), BF16),
        scratch_shapes=[
            pltpu.VMEM((2, FOX_HEADS_PER_STEP, TQ, TK), F32),
            pltpu.VMEM((2, FOX_HEADS_PER_STEP, TQ, TK), BF16),
            pltpu.VMEM((FOX_HEADS_PER_STEP, TQ, LANES), F32),
            pltpu.VMEM((FOX_HEADS_PER_STEP, TQ, LANES), F32),
            pltpu.VMEM((FOX_HEADS_PER_STEP, TQ, LANES), F32),
            pltpu.VMEM((FOX_HEADS_PER_STEP, TQ, 2 * FOX_HEAD_DIM), F32),
        ],
        compiler_params=pltpu.CompilerParams(
            dimension_semantics=("arbitrary", "arbitrary"), vmem_limit_bytes=VMEM_LIMIT),
        name="fox",
    )(z, z, z, ctok, ct)


def _branch_kernel(tiles_per_seq, cb_ref, cc_ref, cv_ref, ccp_ref, cvp_ref, xq_ref,
                   ga_ref, gb_ref, gc_ref, yf_ref, x_ref, mk_ref, mv_ref,
                   cw_ref, cbias_ref, wa_ref, wb_ref, wc_ref, wo_ref, o_ref):
    i = pl.program_id(0)
    tm = x_ref.shape[0]

    u = cc_ref[...].astype(F32) * cv_ref[...].astype(F32)
    first = (i % tiles_per_seq) == 0
    prev = ccp_ref[...].astype(F32) * cvp_ref[...].astype(F32)
    prev = jnp.where(first, 0.0, prev)
    hrow = lax.broadcasted_iota(jnp.int32, (HALO_ROWS, D_MODEL), 0)
    u1 = pltpu.roll(u, 1, 0)
    u2 = pltpu.roll(u, 2, 0)
    u1 = jnp.concatenate(
        [jnp.where(hrow < 1, pltpu.roll(prev, 1, 0), u1[:HALO_ROWS]), u1[HALO_ROWS:]], axis=0)
    u2 = jnp.concatenate(
        [jnp.where(hrow < 2, pltpu.roll(prev, 2, 0), u2[:HALO_ROWS]), u2[HALO_ROWS:]], axis=0)
    conv = cw_ref[0:1, :] * u2 + cw_ref[1:2, :] * u1 + cw_ref[2:3, :] * u
    y_conv = cb_ref[...].astype(F32) * (conv + cbias_ref[...])
    merged = ga_ref[...].astype(F32) * _dot(y_conv.astype(BF16), wa_ref[...])

    merged += gb_ref[...].astype(F32) * _dot(yf_ref[...], wb_ref[...])

    heads = []
    for h in range(XA_HEADS):
        hs = slice(h * XA_HEAD_DIM, (h + 1) * XA_HEAD_DIM)
        s = _dot_nt(xq_ref[:, hs], mk_ref[0, :, hs])
        p = jnp.exp2(s - jnp.max(s, axis=-1, keepdims=True))
        o = _dot(p.astype(BF16), mv_ref[0, :, hs])
        heads.append(o / jnp.sum(p, axis=-1, keepdims=True))
    y_xa = jnp.concatenate(heads, axis=-1).astype(BF16)
    merged += gc_ref[...].astype(F32) * _dot(y_xa, wc_ref[...])

    o_ref[...] = x_ref[...] + _dot(merged.astype(BF16), wo_ref[...])


def _branch(z, y_fox, x2, mk, mv, cw, cbias, wa, wb, wc, wo, seq):
    n, d = x2.shape
    tm = TM_BR
    nt = n // tm
    tps = seq // tm
    n_mem = mk.shape[1]
    hpt = tm // HALO_ROWS

    def zspec(g):
        return pl.BlockSpec((tm, d), lambda i: (i, g))

    def halo(g):
        return pl.BlockSpec((HALO_ROWS, d), lambda i: (jnp.maximum(i * hpt - 1, 0), g))

    kern = functools.partial(_branch_kernel, tps)
    return pl.pallas_call(
        kern,
        grid=(nt,),
        in_specs=[
            zspec(G_CB), zspec(G_CC), zspec(G_CV), halo(G_CC), halo(G_CV), zspec(G_XQ),
            zspec(G_GA), zspec(G_GB), zspec(G_GC),
            pl.BlockSpec((tm, d), lambda i: (i, 0)),
            pl.BlockSpec((tm, d), lambda i: (i, 0)),
            pl.BlockSpec((1, n_mem, d), lambda i: (i // tps, 0, 0)),
            pl.BlockSpec((1, n_mem, d), lambda i: (i // tps, 0, 0)),
            _const_spec((CONV_WIDTH, d)),
            _const_spec((1, d)),
            _const_spec((d, d)), _const_spec((d, d)), _const_spec((d, d)), _const_spec((d, d)),
        ],
        out_specs=pl.BlockSpec((tm, d), lambda i: (i, 0)),
        out_shape=jax.ShapeDtypeStruct((n, d), F32),
        compiler_params=pltpu.CompilerParams(
            dimension_semantics=("arbitrary",), vmem_limit_bytes=VMEM_LIMIT),
        name="branch",
    )(z, z, z, z, z, z, z, z, z, y_fox, x2, mk, mv, cw, cbias, wa, wb, wc, wo)


def _ffn_kernel(x_ref, g_ref, wi_ref, wo_ref, o_ref, a_ref):
    x = x_ref[...]
    h = _rms(x, g_ref[...]).astype(BF16)
    for c in range(D_FF // FF_CHUNK):
        g = _dot(h, wi_ref[:, c * FF_CHUNK:(c + 1) * FF_CHUNK])
        u = _dot(h, wi_ref[:, D_FF + c * FF_CHUNK:D_FF + (c + 1) * FF_CHUNK])
        a_ref[:, c * FF_CHUNK:(c + 1) * FF_CHUNK] = (g * jax.nn.sigmoid(g) * u).astype(BF16)
    o_ref[...] = x + _dot(a_ref[...], wo_ref[...])


def _ffn(x1, g, wi, wo):
    n, d = x1.shape
    tm = TM_FFN
    return pl.pallas_call(
        _ffn_kernel,
        grid=(n // tm,),
        in_specs=[
            pl.BlockSpec((tm, d), lambda i: (i, 0)),
            _const_spec((1, d)),
            _const_spec((d, 2 * D_FF)),
            _const_spec((D_FF, d)),
        ],
        out_specs=pl.BlockSpec((tm, d), lambda i: (i, 0)),
        out_shape=jax.ShapeDtypeStruct((n, d), F32),
        scratch_shapes=[pltpu.VMEM((tm, D_FF), BF16)],
        compiler_params=pltpu.CompilerParams(
            dimension_semantics=("arbitrary",), vmem_limit_bytes=VMEM_LIMIT),
        name="ffn",
    )(x1, g, wi, wo)


def kernel(x, mem, norm1_g, w_in, conv_w, conv_b, fox_f_bias, fox_q_g, fox_k_g, mem_norm_g,
           w_mem_kv, xa_q_g, xa_k_g, w_br_conv, w_br_fox, w_br_xa, w_o, norm2_g, w_ffn_in,
           w_ffn_out):
    batch, seq, d = x.shape
    depth = norm1_g.shape[0]
    n_main = N_GROUPS * d
    x2 = x.reshape(batch * seq, d)
    for l in range(depth):
        w_main = w_in[l].astype(BF16)
        w_ff = jnp.pad(w_main[:, n_main:], ((0, 0), (0, LANES - FOX_HEADS)))
        f_bias = jnp.pad(fox_f_bias[l], (0, LANES - FOX_HEADS)).reshape(1, LANES)

        mk, mv = _mem_kv(mem, mem_norm_g[l].reshape(1, d), w_mem_kv[l].astype(BF16),
                         xa_k_g[l].reshape(1, XA_HEAD_DIM))
        z, ctok, ct = _in_proj(x2, norm1_g[l].reshape(1, d), w_main, w_ff, f_bias,
                               fox_q_g[l].reshape(1, FOX_HEAD_DIM),
                               fox_k_g[l].reshape(1, FOX_HEAD_DIM),
                               xa_q_g[l].reshape(1, XA_HEAD_DIM), seq)
        y_fox = _fox(z, ctok, ct, batch, seq)
        x1 = _branch(z, y_fox, x2, mk, mv, conv_w[l], conv_b[l].reshape(1, d),
                     w_br_conv[l].astype(BF16), w_br_fox[l].astype(BF16),
                     w_br_xa[l].astype(BF16), w_o[l].astype(BF16), seq)
        x2 = _ffn(x1, norm2_g[l].reshape(1, d), w_ffn_in[l].astype(BF16),
                  w_ffn_out[l].astype(BF16))
    return x2.reshape(batch, seq, d)
```

```python
import functools
import math

import jax
import jax.numpy as jnp
from jax import lax
from jax.experimental import pallas as pl
from jax.experimental.pallas import tpu as pltpu

F32 = jnp.float32
BF16 = jnp.bfloat16

D_MODEL = 1024
N_GROUPS = 10
FOX_HEADS = 8
FOX_HEAD_DIM = 128
XA_HEADS = 4
XA_HEAD_DIM = 256
D_FF = 2816
FF_CHUNK = 256
CONV_WIDTH = 3
EPS = 1e-6
LOG2E = math.log2(math.e)
LANES = 128
HALO_ROWS = 16

G_CB, G_CC, G_CV, G_FQ, G_FK, G_FV, G_XQ, G_GA, G_GB, G_GC = range(N_GROUPS)
G_QX, G_KX = N_GROUPS, N_GROUPS + 1
N_Z_GROUPS = N_GROUPS + 2

TM_IN = 512
IN_CHUNK = 256
TQ = 512
TK = 512
FOX_HEADS_PER_STEP = 2
FOX_ROW_CHUNK = 64
TM_BR = 512
TM_FFN = 512
VMEM_LIMIT = 56 * 1024 * 1024
VMEM_LIMIT_IN_PROJ = 60 * 1024 * 1024


def _rms(x, g):
    ms = jnp.mean(x * x, axis=-1, keepdims=True)
    return x * lax.rsqrt(ms + EPS) * g


def _dot(a, b):
    return jnp.dot(a, b, preferred_element_type=F32)


def _dot_nt(a, b):
    return lax.dot_general(a, b, (((1,), (1,)), ((), ())), preferred_element_type=F32)


def _const_spec(shape):
    nd = len(shape)
    return pl.BlockSpec(shape, lambda *_: (0,) * nd, pipeline_mode=pl.Buffered(1))


def _mem_kv_kernel(mem_ref, g_ref, w_ref, kg_ref, k_ref, v_ref):
    mn = _rms(mem_ref[0], g_ref[...]).astype(BF16)
    kv = _dot(mn, w_ref[...])
    for h in range(XA_HEADS):
        hs = slice(h * XA_HEAD_DIM, (h + 1) * XA_HEAD_DIM)
        k_ref[0, :, hs] = _rms(kv[:, hs], kg_ref[...]).astype(BF16)
    v_ref[0] = kv[:, D_MODEL:].astype(BF16)


def _mem_kv(mem, g, w, kg):
    b, m, d = mem.shape
    return pl.pallas_call(
        _mem_kv_kernel,
        grid=(b,),
        in_specs=[
            pl.BlockSpec((1, m, d), lambda i: (i, 0, 0)),
            _const_spec((1, d)),
            _const_spec((d, 2 * d)),
            _const_spec((1, XA_HEAD_DIM)),
        ],
        out_specs=[
            pl.BlockSpec((1, m, d), lambda i: (i, 0, 0)),
            pl.BlockSpec((1, m, d), lambda i: (i, 0, 0)),
        ],
        out_shape=[jax.ShapeDtypeStruct((b, m, d), BF16)] * 2,
        compiler_params=pltpu.CompilerParams(
            dimension_semantics=("arbitrary",), vmem_limit_bytes=VMEM_LIMIT),
        name="mem_kv",
    )(mem, g, w, kg)


def _log_sigmoid(x):
    return jnp.minimum(x, 0.0) - jnp.log1p(jnp.exp(-jnp.abs(x)))


def _cumsum_rows(x):
    n = x.shape[0]
    row = lax.broadcasted_iota(jnp.int32, x.shape, 0)
    s = 1
    while s < n:
        x = x + jnp.where(row >= s, pltpu.roll(x, s, 0), 0.0)
        s *= 2
    return x


def _in_proj_kernel(tiles_per_seq, x_ref, g1_ref, w_ref, wff_ref, fb_ref, qg_ref, kg_ref,
                    xqg_ref, z_ref, carry_ref, h_ref):
    i = pl.program_id(0)
    tm = x_ref.shape[0]
    d = x_ref.shape[1]
    h_ref[...] = _rms(x_ref[...], g1_ref[...]).astype(BF16)

    lf = _log_sigmoid(_dot(h_ref[...], wff_ref[...]) + fb_ref[...])

    @pl.when(i % tiles_per_seq == 0)
    def _():
        carry_ref[...] = jnp.zeros_like(carry_ref)

    c = _cumsum_rows(lf) + carry_ref[...]
    carry_ref[...] = c[tm - 1:tm, :]
    c = c * LOG2E

    hi = c.astype(BF16).astype(F32)
    mid = (c - hi).astype(BF16).astype(F32)
    lo = (c - hi - mid).astype(BF16).astype(F32)
    lane = lax.broadcasted_iota(jnp.int32, (tm, LANES), 1)
    q_const = jnp.where(lane < 6, 1.0, 0.0)
    k_const = jnp.where(lane < 3, 1.0, 0.0)

    def decay_columns(h):
        col = lambda a: jnp.broadcast_to(a[:, h:h + 1], (tm, LANES))
        parts = jnp.where(lane % 3 == 0, col(hi), jnp.where(lane % 3 == 1, col(mid), col(lo)))
        z_ref[:, G_QX * d + h * LANES:G_QX * d + (h + 1) * LANES] = jnp.where(
            lane < 3, parts, q_const).astype(BF16)
        z_ref[:, G_KX * d + h * LANES:G_KX * d + (h + 1) * LANES] = jnp.where(
            (lane >= 3) & (lane < 6), -parts, k_const).astype(BF16)

    def head_norm(r, g_ref, scale, col0):
        hd = g_ref.shape[1]
        for h in range(r.shape[1] // hd):
            z_ref[:, col0 + h * hd:col0 + (h + 1) * hd] = (
                _rms(r[:, h * hd:(h + 1) * hd], g_ref[...]) * scale).astype(BF16)

    n_chunks = N_GROUPS * d // IN_CHUNK
    for ci in range(n_chunks):
        col0 = ci * IN_CHUNK
        g = col0 // d
        if ci % (n_chunks // FOX_HEADS) == 0:
            decay_columns(ci // (n_chunks // FOX_HEADS))
        r = _dot(h_ref[...], w_ref[:, col0:col0 + IN_CHUNK])
        if g == G_FQ:
            head_norm(r, qg_ref, LOG2E / math.sqrt(FOX_HEAD_DIM), col0)
        elif g == G_FK:
            head_norm(r, kg_ref, 1.0, col0)
        elif g == G_XQ:
            head_norm(r, xqg_ref, LOG2E / math.sqrt(XA_HEAD_DIM), col0)
        elif g in (G_GA, G_GB, G_GC):
            z_ref[:, col0:col0 + IN_CHUNK] = jax.nn.sigmoid(r).astype(BF16)
        else:
            z_ref[:, col0:col0 + IN_CHUNK] = r.astype(BF16)


def _in_proj(x2, g1, w, wff, fb, qg, kg, xqg, seq):
    n, d = x2.shape
    tm = TM_IN
    kern = functools.partial(_in_proj_kernel, seq // tm)
    return pl.pallas_call(
        kern,
        grid=(n // tm,),
        in_specs=[
            pl.BlockSpec((tm, d), lambda i: (i, 0)),
            _const_spec((1, d)),
            _const_spec(w.shape),
            _const_spec((d, LANES)),
            _const_spec((1, LANES)),
            _const_spec((1, FOX_HEAD_DIM)),
            _const_spec((1, FOX_HEAD_DIM)),
            _const_spec((1, XA_HEAD_DIM)),
        ],
        out_specs=pl.BlockSpec((tm, N_Z_GROUPS * d), lambda i: (i, 0)),
        out_shape=jax.ShapeDtypeStruct((n, N_Z_GROUPS * d), BF16),
        scratch_shapes=[pltpu.VMEM((1, LANES), F32), pltpu.VMEM((tm, d), BF16)],
        compiler_params=pltpu.CompilerParams(
            dimension_semantics=("arbitrary",), vmem_limit_bytes=VMEM_LIMIT_IN_PROJ),
        name="in_proj",
    )(x2, g1, w, wff, fb, qg, kg, xqg)


def _fox_kernel(q_ref, qx_ref, k_ref, kx_ref, v_ref, o_ref,
                t_ref, p_ref, m_ref, alpha_ref, acc_ref):
    qi = pl.program_id(1)
    row = lax.broadcasted_iota(jnp.int32, (TQ, TK), 0)
    col = lax.broadcasted_iota(jnp.int32, (TQ, TK), 1)
    causal = col <= row
    ones = jnp.ones((TK, FOX_HEAD_DIM), BF16)

    def row_parts(diagonal):
        if diagonal:
            return ((0, TQ // 2, TK // 2), (TQ // 2, TQ, TK))
        return ((0, TQ, TK),)

    def logits(h, s, kb, diagonal):
        hs = slice(h * FOX_HEAD_DIM, (h + 1) * FOX_HEAD_DIM)
        for r0, r1, width in row_parts(diagonal):
            keys = slice(kb * TK, kb * TK + width)
            t = _dot_nt(jnp.concatenate([q_ref[r0:r1, hs], qx_ref[r0:r1, hs]], axis=1),
                        jnp.concatenate([k_ref[keys, hs], kx_ref[keys, hs]], axis=1))
            if diagonal:
                t = jnp.where(causal[r0:r1, :width], t, -jnp.inf)
            t_ref[kb % 2, s, r0:r1, :width] = t

    def accumulate(h, s, kb, diagonal):
        hs = slice(h * FOX_HEAD_DIM, (h + 1) * FOX_HEAD_DIM)
        par = kb % 2
        for r0, r1, width in row_parts(diagonal):
            for r in range(r0, r1, FOX_ROW_CHUNK):
                rows = slice(r, r + FOX_ROW_CHUNK)
                t = t_ref[par, s, rows, :width]
                m_new = jnp.broadcast_to(
                    jnp.max(t, axis=-1, keepdims=True), (FOX_ROW_CHUNK, LANES))
                if kb > 0:
                    m_old = m_ref[s, rows, :]
                    m_new = jnp.maximum(m_old, m_new)
                    alpha_ref[s, rows, :] = jnp.exp2(m_old - m_new)
                m_ref[s, rows, :] = m_new
                p_ref[par, s, rows, :width] = jnp.exp2(
                    t - jnp.tile(m_new, (1, width // LANES))).astype(BF16)
            keys = slice(kb * TK, kb * TK + width)
            v_ext = jnp.concatenate([v_ref[keys, hs], ones[:width]], axis=1)
            pv = _dot(p_ref[par, s, r0:r1, :width], v_ext)
            if kb > 0:
                pv += jnp.tile(alpha_ref[s, r0:r1, :], (1, 2)) * acc_ref[s, r0:r1, :]
            acc_ref[s, r0:r1, :] = pv

    def q_tile(n):
        for h0 in range(0, FOX_HEADS, FOX_HEADS_PER_STEP):
            heads = list(enumerate(range(h0, h0 + FOX_HEADS_PER_STEP)))
            for s, h in heads:
                logits(h, s, 0, n == 0)
            for kb in range(n + 1):
                if kb < n:
                    for s, h in heads:
                        logits(h, s, kb + 1, kb + 1 == n)
                for s, h in heads:
                    accumulate(h, s, kb, kb == n)
            for s, h in heads:
                hs = slice(h * FOX_HEAD_DIM, (h + 1) * FOX_HEAD_DIM)
                o_ref[:, hs] = (acc_ref[s, :, :FOX_HEAD_DIM]
                                / acc_ref[s, :, FOX_HEAD_DIM:]).astype(BF16)

    for n in range(k_ref.shape[0] // TQ):
        pl.when(qi == n)(functools.partial(q_tile, n))


def _fox(z, batch, seq):
    n = z.shape[0]
    d = D_MODEL
    nq = seq // TQ
    return pl.pallas_call(
        _fox_kernel,
        grid=(batch, nq),
        in_specs=[
            pl.BlockSpec((TQ, d), lambda b, q: (b * nq + q, G_FQ)),
            pl.BlockSpec((TQ, d), lambda b, q: (b * nq + q, G_QX)),
            pl.BlockSpec((seq, d), lambda b, q: (b, G_FK)),
            pl.BlockSpec((seq, d), lambda b, q: (b, G_KX)),
            pl.BlockSpec((seq, d), lambda b, q: (b, G_FV)),
        ],
        out_specs=pl.BlockSpec((TQ, d), lambda b, q: (b * nq + q, 0)),
        out_shape=jax.ShapeDtypeStruct((n, d), BF16),
        scratch_shapes=[
            pltpu.VMEM((2, FOX_HEADS_PER_STEP, TQ, TK), F32),
            pltpu.VMEM((2, FOX_HEADS_PER_STEP, TQ, TK), BF16),
            pltpu.VMEM((FOX_HEADS_PER_STEP, TQ, LANES), F32),
            pltpu.VMEM((FOX_HEADS_PER_STEP, TQ, LANES), F32),
            pltpu.VMEM((FOX_HEADS_PER_STEP, TQ, 2 * FOX_HEAD_DIM), F32),
        ],
        compiler_params=pltpu.CompilerParams(
            dimension_semantics=("arbitrary", "arbitrary"), vmem_limit_bytes=VMEM_LIMIT),
        name="fox",
    )(z, z, z, z, z)


def _branch_kernel(tiles_per_seq, cb_ref, cc_ref, cv_ref, ccp_ref, cvp_ref, xq_ref,
                   ga_ref, gb_ref, gc_ref, yf_ref, x_ref, mk_ref, mv_ref,
                   cw_ref, cbias_ref, wa_ref, wb_ref, wc_ref, wo_ref, o_ref):
    i = pl.program_id(0)
    tm = x_ref.shape[0]

    u = cc_ref[...].astype(F32) * cv_ref[...].astype(F32)
    first = (i % tiles_per_seq) == 0
    prev = ccp_ref[...].astype(F32) * cvp_ref[...].astype(F32)
    prev = jnp.where(first, 0.0, prev)
    hrow = lax.broadcasted_iota(jnp.int32, (HALO_ROWS, D_MODEL), 0)
    u1 = pltpu.roll(u, 1, 0)
    u2 = pltpu.roll(u, 2, 0)
    u1 = jnp.concatenate(
        [jnp.where(hrow < 1, pltpu.roll(prev, 1, 0), u1[:HALO_ROWS]), u1[HALO_ROWS:]], axis=0)
    u2 = jnp.concatenate(
        [jnp.where(hrow < 2, pltpu.roll(prev, 2, 0), u2[:HALO_ROWS]), u2[HALO_ROWS:]], axis=0)
    conv = cw_ref[0:1, :] * u2 + cw_ref[1:2, :] * u1 + cw_ref[2:3, :] * u
    y_conv = cb_ref[...].astype(F32) * (conv + cbias_ref[...])
    merged = ga_ref[...].astype(F32) * _dot(y_conv.astype(BF16), wa_ref[...])

    merged += gb_ref[...].astype(F32) * _dot(yf_ref[...], wb_ref[...])

    heads = []
    for h in range(XA_HEADS):
        hs = slice(h * XA_HEAD_DIM, (h + 1) * XA_HEAD_DIM)
        s = _dot_nt(xq_ref[:, hs], mk_ref[0, :, hs])
        p = jnp.exp2(s - jnp.max(s, axis=-1, keepdims=True))
        o = _dot(p.astype(BF16), mv_ref[0, :, hs])
        heads.append(o / jnp.sum(p, axis=-1, keepdims=True))
    y_xa = jnp.concatenate(heads, axis=-1).astype(BF16)
    merged += gc_ref[...].astype(F32) * _dot(y_xa, wc_ref[...])

    o_ref[...] = x_ref[...] + _dot(merged.astype(BF16), wo_ref[...])


def _branch(z, y_fox, x2, mk, mv, cw, cbias, wa, wb, wc, wo, seq):
    n, d = x2.shape
    tm = TM_BR
    nt = n // tm
    tps = seq // tm
    n_mem = mk.shape[1]
    hpt = tm // HALO_ROWS

    def zspec(g):
        return pl.BlockSpec((tm, d), lambda i: (i, g))

    def halo(g):
        return pl.BlockSpec((HALO_ROWS, d), lambda i: (jnp.maximum(i * hpt - 1, 0), g))

    kern = functools.partial(_branch_kernel, tps)
    return pl.pallas_call(
        kern,
        grid=(nt,),
        in_specs=[
            zspec(G_CB), zspec(G_CC), zspec(G_CV), halo(G_CC), halo(G_CV), zspec(G_XQ),
            zspec(G_GA), zspec(G_GB), zspec(G_GC),
            pl.BlockSpec((tm, d), lambda i: (i, 0)),
            pl.BlockSpec((tm, d), lambda i: (i, 0)),
            pl.BlockSpec((1, n_mem, d), lambda i: (i // tps, 0, 0)),
            pl.BlockSpec((1, n_mem, d), lambda i: (i // tps, 0, 0)),
            _const_spec((CONV_WIDTH, d)),
            _const_spec((1, d)),
            _const_spec((d, d)), _const_spec((d, d)), _const_spec((d, d)), _const_spec((d, d)),
        ],
        out_specs=pl.BlockSpec((tm, d), lambda i: (i, 0)),
        out_shape=jax.ShapeDtypeStruct((n, d), F32),
        compiler_params=pltpu.CompilerParams(
            dimension_semantics=("arbitrary",), vmem_limit_bytes=VMEM_LIMIT),
        name="branch",
    )(z, z, z, z, z, z, z, z, z, y_fox, x2, mk, mv, cw, cbias, wa, wb, wc, wo)


def _ffn_kernel(x_ref, g_ref, wi_ref, wo_ref, o_ref, a_ref):
    x = x_ref[...]
    h = _rms(x, g_ref[...]).astype(BF16)
    for c in range(D_FF // FF_CHUNK):
        g = _dot(h, wi_ref[:, c * FF_CHUNK:(c + 1) * FF_CHUNK])
        u = _dot(h, wi_ref[:, D_FF + c * FF_CHUNK:D_FF + (c + 1) * FF_CHUNK])
        a_ref[:, c * FF_CHUNK:(c + 1) * FF_CHUNK] = (g * jax.nn.sigmoid(g) * u).astype(BF16)
    o_ref[...] = x + _dot(a_ref[...], wo_ref[...])


def _ffn(x1, g, wi, wo):
    n, d = x1.shape
    tm = TM_FFN
    return pl.pallas_call(
        _ffn_kernel,
        grid=(n // tm,),
        in_specs=[
            pl.BlockSpec((tm, d), lambda i: (i, 0)),
            _const_spec((1, d)),
            _const_spec((d, 2 * D_FF)),
            _const_spec((D_FF, d)),
        ],
        out_specs=pl.BlockSpec((tm, d), lambda i: (i, 0)),
        out_shape=jax.ShapeDtypeStruct((n, d), F32),
        scratch_shapes=[pltpu.VMEM((tm, D_FF), BF16)],
        compiler_params=pltpu.CompilerParams(
            dimension_semantics=("arbitrary",), vmem_limit_bytes=VMEM_LIMIT),
        name="ffn",
    )(x1, g, wi, wo)


def kernel(x, mem, norm1_g, w_in, conv_w, conv_b, fox_f_bias, fox_q_g, fox_k_g, mem_norm_g,
           w_mem_kv, xa_q_g, xa_k_g, w_br_conv, w_br_fox, w_br_xa, w_o, norm2_g, w_ffn_in,
           w_ffn_out):
    batch, seq, d = x.shape
    depth = norm1_g.shape[0]
    n_main = N_GROUPS * d
    x2 = x.reshape(batch * seq, d)
    for l in range(depth):
        w_main = w_in[l].astype(BF16)
        w_ff = jnp.pad(w_main[:, n_main:], ((0, 0), (0, LANES - FOX_HEADS)))
        f_bias = jnp.pad(fox_f_bias[l], (0, LANES - FOX_HEADS)).reshape(1, LANES)

        mk, mv = _mem_kv(mem, mem_norm_g[l].reshape(1, d), w_mem_kv[l].astype(BF16),
                         xa_k_g[l].reshape(1, XA_HEAD_DIM))
        z = _in_proj(x2, norm1_g[l].reshape(1, d), w_main, w_ff, f_bias,
                               fox_q_g[l].reshape(1, FOX_HEAD_DIM),
                               fox_k_g[l].reshape(1, FOX_HEAD_DIM),
                               xa_q_g[l].reshape(1, XA_HEAD_DIM), seq)
        y_fox = _fox(z, batch, seq)
        x1 = _branch(z, y_fox, x2, mk, mv, conv_w[l], conv_b[l].reshape(1, d),
                     w_br_conv[l].astype(BF16), w_br_fox[l].astype(BF16),
                     w_br_xa[l].astype(BF16), w_o[l].astype(BF16), seq)
        x2 = _ffn(x1, norm2_g[l].reshape(1, d), w_ffn_in[l].astype(BF16),
                  w_ffn_out[l].astype(BF16))
    return x2.reshape(batch, seq, d)
```

```python
import functools
import math

import jax
import jax.numpy as jnp
from jax import lax
from jax.experimental import pallas as pl
from jax.experimental.pallas import tpu as pltpu

F32 = jnp.float32
BF16 = jnp.bfloat16

D_MODEL = 1024
N_GROUPS = 10
FOX_HEADS = 8
FOX_HEAD_DIM = 128
XA_HEADS = 4
XA_HEAD_DIM = 256
D_FF = 2816
FF_CHUNK = 256
CONV_WIDTH = 3
EPS = 1e-6
LOG2E = math.log2(math.e)
LANES = 128
HALO_ROWS = 16

W_CB, W_CC, W_CV, W_FQ, W_FK, W_FV, W_XQ, W_GA, W_GB, W_GC = range(N_GROUPS)
G_CB, G_U, G_FQ, G_FK, G_FV, G_XQ, G_GA, G_GB, G_GC, G_QX, G_KX = range(N_GROUPS + 1)
N_Z_GROUPS = N_GROUPS + 1

TM_IN = 512
IN_CHUNK = 256
TQ = 512
TK = 512
FOX_HEADS_PER_STEP = 2
FOX_ROW_CHUNK = 64
TM_BR = 512
TM_FFN = 1024
VMEM_LIMIT = 56 * 1024 * 1024
VMEM_LIMIT_IN_PROJ = 60 * 1024 * 1024


def _rms(x, g):
    ms = jnp.mean(x * x, axis=-1, keepdims=True)
    return x * lax.rsqrt(ms + EPS) * g


def _dot(a, b):
    return jnp.dot(a, b, preferred_element_type=F32)


def _dot_nt(a, b):
    return lax.dot_general(a, b, (((1,), (1,)), ((), ())), preferred_element_type=F32)


def _const_spec(shape):
    nd = len(shape)
    return pl.BlockSpec(shape, lambda *_: (0,) * nd, pipeline_mode=pl.Buffered(1))


def _mem_kv_kernel(mem_ref, g_ref, w_ref, kg_ref, k_ref, v_ref):
    mn = _rms(mem_ref[0], g_ref[...]).astype(BF16)
    kv = _dot(mn, w_ref[...])
    for h in range(XA_HEADS):
        hs = slice(h * XA_HEAD_DIM, (h + 1) * XA_HEAD_DIM)
        k_ref[0, :, hs] = _rms(kv[:, hs], kg_ref[...]).astype(BF16)
    v_ref[0] = kv[:, D_MODEL:].astype(BF16)


def _mem_kv(mem, g, w, kg):
    b, m, d = mem.shape
    return pl.pallas_call(
        _mem_kv_kernel,
        grid=(b,),
        in_specs=[
            pl.BlockSpec((1, m, d), lambda i: (i, 0, 0)),
            _const_spec((1, d)),
            _const_spec((d, 2 * d)),
            _const_spec((1, XA_HEAD_DIM)),
        ],
        out_specs=[
            pl.BlockSpec((1, m, d), lambda i: (i, 0, 0)),
            pl.BlockSpec((1, m, d), lambda i: (i, 0, 0)),
        ],
        out_shape=[jax.ShapeDtypeStruct((b, m, d), BF16)] * 2,
        compiler_params=pltpu.CompilerParams(
            dimension_semantics=("arbitrary",), vmem_limit_bytes=VMEM_LIMIT),
        name="mem_kv",
    )(mem, g, w, kg)


def _log_sigmoid(x):
    return jnp.minimum(x, 0.0) - jnp.log1p(jnp.exp(-jnp.abs(x)))


def _cumsum_rows(x):
    n = x.shape[0]
    row = lax.broadcasted_iota(jnp.int32, x.shape, 0)
    s = 1
    while s < n:
        x = x + jnp.where(row >= s, pltpu.roll(x, s, 0), 0.0)
        s *= 2
    return x


def _in_proj_kernel(tiles_per_seq, x_ref, g1_ref, w_ref, wff_ref, fb_ref, qg_ref, kg_ref,
                    xqg_ref, z_ref, carry_ref, h_ref, cc_ref):
    i = pl.program_id(0)
    tm = x_ref.shape[0]
    d = x_ref.shape[1]
    h_ref[...] = _rms(x_ref[...], g1_ref[...]).astype(BF16)

    lf = _log_sigmoid(_dot(h_ref[...], wff_ref[...]) + fb_ref[...])

    @pl.when(i % tiles_per_seq == 0)
    def _():
        carry_ref[...] = jnp.zeros_like(carry_ref)

    c = _cumsum_rows(lf) + carry_ref[...]
    carry_ref[...] = c[tm - 1:tm, :]
    c = c * LOG2E

    hi = c.astype(BF16).astype(F32)
    mid = (c - hi).astype(BF16).astype(F32)
    lo = (c - hi - mid).astype(BF16).astype(F32)
    lane = lax.broadcasted_iota(jnp.int32, (tm, LANES), 1)
    q_const = jnp.where(lane < 6, 1.0, 0.0)
    k_const = jnp.where(lane < 3, 1.0, 0.0)

    def decay_columns(h):
        col = lambda a: jnp.broadcast_to(a[:, h:h + 1], (tm, LANES))
        parts = jnp.where(lane % 3 == 0, col(hi), jnp.where(lane % 3 == 1, col(mid), col(lo)))
        z_ref[:, G_QX * d + h * LANES:G_QX * d + (h + 1) * LANES] = jnp.where(
            lane < 3, parts, q_const).astype(BF16)
        z_ref[:, G_KX * d + h * LANES:G_KX * d + (h + 1) * LANES] = jnp.where(
            (lane >= 3) & (lane < 6), -parts, k_const).astype(BF16)

    def head_norm(r, g_ref, scale, col0):
        hd = g_ref.shape[1]
        for h in range(r.shape[1] // hd):
            z_ref[:, col0 + h * hd:col0 + (h + 1) * hd] = (
                _rms(r[:, h * hd:(h + 1) * hd], g_ref[...]) * scale).astype(BF16)

    n_chunks = N_GROUPS * d // IN_CHUNK
    for ci in range(n_chunks):
        col0 = ci * IN_CHUNK
        wg = col0 // d
        cs = slice(col0 % d, col0 % d + IN_CHUNK)
        zcol0 = col0 if wg == W_CB else col0 - d
        if ci % (n_chunks // FOX_HEADS) == 0:
            decay_columns(ci // (n_chunks // FOX_HEADS))
        r = _dot(h_ref[...], w_ref[:, col0:col0 + IN_CHUNK])
        if wg == W_CC:
            cc_ref[:, cs] = r
        elif wg == W_CV:
            z_ref[:, zcol0:zcol0 + IN_CHUNK] = (cc_ref[:, cs] * r).astype(BF16)
        elif wg == W_FQ:
            head_norm(r, qg_ref, LOG2E / math.sqrt(FOX_HEAD_DIM), zcol0)
        elif wg == W_FK:
            head_norm(r, kg_ref, 1.0, zcol0)
        elif wg == W_XQ:
            head_norm(r, xqg_ref, LOG2E / math.sqrt(XA_HEAD_DIM), zcol0)
        elif wg in (W_GA, W_GB, W_GC):
            z_ref[:, zcol0:zcol0 + IN_CHUNK] = jax.nn.sigmoid(r).astype(BF16)
        else:
            z_ref[:, zcol0:zcol0 + IN_CHUNK] = r.astype(BF16)


def _in_proj(x2, g1, w, wff, fb, qg, kg, xqg, seq):
    n, d = x2.shape
    tm = TM_IN
    kern = functools.partial(_in_proj_kernel, seq // tm)
    return pl.pallas_call(
        kern,
        grid=(n // tm,),
        in_specs=[
            pl.BlockSpec((tm, d), lambda i: (i, 0)),
            _const_spec((1, d)),
            _const_spec(w.shape),
            _const_spec((d, LANES)),
            _const_spec((1, LANES)),
            _const_spec((1, FOX_HEAD_DIM)),
            _const_spec((1, FOX_HEAD_DIM)),
            _const_spec((1, XA_HEAD_DIM)),
        ],
        out_specs=pl.BlockSpec((tm, N_Z_GROUPS * d), lambda i: (i, 0)),
        out_shape=jax.ShapeDtypeStruct((n, N_Z_GROUPS * d), BF16),
        scratch_shapes=[
            pltpu.VMEM((1, LANES), F32),
            pltpu.VMEM((tm, d), BF16),
            pltpu.VMEM((tm, d), F32),
        ],
        compiler_params=pltpu.CompilerParams(
            dimension_semantics=("arbitrary",), vmem_limit_bytes=VMEM_LIMIT_IN_PROJ),
        name="in_proj",
    )(x2, g1, w, wff, fb, qg, kg, xqg)


def _fox_kernel(q_ref, qx_ref, k_ref, kx_ref, v_ref, o_ref,
                t_ref, p_ref, m_ref, alpha_ref, acc_ref):
    qi = pl.program_id(1)
    row = lax.broadcasted_iota(jnp.int32, (TQ, TK), 0)
    col = lax.broadcasted_iota(jnp.int32, (TQ, TK), 1)
    causal = col <= row
    ones = jnp.ones((TK, FOX_HEAD_DIM), BF16)

    def row_parts(diagonal):
        if diagonal:
            return ((0, TQ // 2, TK // 2), (TQ // 2, TQ, TK))
        return ((0, TQ, TK),)

    def logits(h, s, kb, diagonal):
        hs = slice(h * FOX_HEAD_DIM, (h + 1) * FOX_HEAD_DIM)
        for r0, r1, width in row_parts(diagonal):
            keys = slice(kb * TK, kb * TK + width)
            t = _dot_nt(jnp.concatenate([q_ref[r0:r1, hs], qx_ref[r0:r1, hs]], axis=1),
                        jnp.concatenate([k_ref[keys, hs], kx_ref[keys, hs]], axis=1))
            if diagonal:
                t = jnp.where(causal[r0:r1, :width], t, -jnp.inf)
            t_ref[kb % 2, s, r0:r1, :width] = t

    def accumulate(h, s, kb, diagonal):
        hs = slice(h * FOX_HEAD_DIM, (h + 1) * FOX_HEAD_DIM)
        par = kb % 2
        for r0, r1, width in row_parts(diagonal):
            for r in range(r0, r1, FOX_ROW_CHUNK):
                rows = slice(r, r + FOX_ROW_CHUNK)
                t = t_ref[par, s, rows, :width]
                m_new = jnp.broadcast_to(
                    jnp.max(t, axis=-1, keepdims=True), (FOX_ROW_CHUNK, LANES))
                if kb > 0:
                    m_old = m_ref[s, rows, :]
                    m_new = jnp.maximum(m_old, m_new)
                    alpha_ref[s, rows, :] = jnp.exp2(m_old - m_new)
                m_ref[s, rows, :] = m_new
                p_ref[par, s, rows, :width] = jnp.exp2(
                    t - jnp.tile(m_new, (1, width // LANES))).astype(BF16)
            keys = slice(kb * TK, kb * TK + width)
            v_ext = jnp.concatenate([v_ref[keys, hs], ones[:width]], axis=1)
            pv = _dot(p_ref[par, s, r0:r1, :width], v_ext)
            if kb > 0:
                pv += jnp.tile(alpha_ref[s, r0:r1, :], (1, 2)) * acc_ref[s, r0:r1, :]
            acc_ref[s, r0:r1, :] = pv

    def q_tile(n):
        for h0 in range(0, FOX_HEADS, FOX_HEADS_PER_STEP):
            heads = list(enumerate(range(h0, h0 + FOX_HEADS_PER_STEP)))
            for s, h in heads:
                logits(h, s, 0, n == 0)
            for kb in range(n + 1):
                if kb < n:
                    for s, h in heads:
                        logits(h, s, kb + 1, kb + 1 == n)
                for s, h in heads:
                    accumulate(h, s, kb, kb == n)
            for s, h in heads:
                hs = slice(h * FOX_HEAD_DIM, (h + 1) * FOX_HEAD_DIM)
                o_ref[:, hs] = (acc_ref[s, :, :FOX_HEAD_DIM]
                                / acc_ref[s, :, FOX_HEAD_DIM:]).astype(BF16)

    for n in range(k_ref.shape[0] // TQ):
        pl.when(qi == n)(functools.partial(q_tile, n))


def _fox(z, batch, seq):
    n = z.shape[0]
    d = D_MODEL
    nq = seq // TQ
    return pl.pallas_call(
        _fox_kernel,
        grid=(batch, nq),
        in_specs=[
            pl.BlockSpec((TQ, d), lambda b, q: (b * nq + q, G_FQ)),
            pl.BlockSpec((TQ, d), lambda b, q: (b * nq + q, G_QX)),
            pl.BlockSpec((seq, d), lambda b, q: (b, G_FK)),
            pl.BlockSpec((seq, d), lambda b, q: (b, G_KX)),
            pl.BlockSpec((seq, d), lambda b, q: (b, G_FV)),
        ],
        out_specs=pl.BlockSpec((TQ, d), lambda b, q: (b * nq + q, 0)),
        out_shape=jax.ShapeDtypeStruct((n, d), BF16),
        scratch_shapes=[
            pltpu.VMEM((2, FOX_HEADS_PER_STEP, TQ, TK), F32),
            pltpu.VMEM((2, FOX_HEADS_PER_STEP, TQ, TK), BF16),
            pltpu.VMEM((FOX_HEADS_PER_STEP, TQ, LANES), F32),
            pltpu.VMEM((FOX_HEADS_PER_STEP, TQ, LANES), F32),
            pltpu.VMEM((FOX_HEADS_PER_STEP, TQ, 2 * FOX_HEAD_DIM), F32),
        ],
        compiler_params=pltpu.CompilerParams(
            dimension_semantics=("arbitrary", "arbitrary"), vmem_limit_bytes=VMEM_LIMIT),
        name="fox",
    )(z, z, z, z, z)


def _branch_kernel(tiles_per_seq, cb_ref, u_ref, up_ref, xq_ref,
                   ga_ref, gb_ref, gc_ref, yf_ref, x_ref, mk_ref, mv_ref,
                   cw_ref, cbias_ref, wa_ref, wb_ref, wc_ref, wo_ref, o_ref):
    i = pl.program_id(0)

    u = u_ref[...].astype(F32)
    first = (i % tiles_per_seq) == 0
    prev = jnp.where(first, 0.0, up_ref[...].astype(F32))
    hrow = lax.broadcasted_iota(jnp.int32, (HALO_ROWS, D_MODEL), 0)
    u1 = pltpu.roll(u, 1, 0)
    u2 = pltpu.roll(u, 2, 0)
    u1 = jnp.concatenate(
        [jnp.where(hrow < 1, pltpu.roll(prev, 1, 0), u1[:HALO_ROWS]), u1[HALO_ROWS:]], axis=0)
    u2 = jnp.concatenate(
        [jnp.where(hrow < 2, pltpu.roll(prev, 2, 0), u2[:HALO_ROWS]), u2[HALO_ROWS:]], axis=0)
    conv = cw_ref[0:1, :] * u2 + cw_ref[1:2, :] * u1 + cw_ref[2:3, :] * u
    y_conv = cb_ref[...].astype(F32) * (conv + cbias_ref[...])
    merged = ga_ref[...].astype(F32) * _dot(y_conv.astype(BF16), wa_ref[...])

    merged += gb_ref[...].astype(F32) * _dot(yf_ref[...], wb_ref[...])

    heads = []
    for h in range(XA_HEADS):
        hs = slice(h * XA_HEAD_DIM, (h + 1) * XA_HEAD_DIM)
        s = _dot_nt(xq_ref[:, hs], mk_ref[0, :, hs])
        p = jnp.exp2(s - jnp.max(s, axis=-1, keepdims=True))
        o = _dot(p.astype(BF16), mv_ref[0, :, hs])
        heads.append(o / jnp.sum(p, axis=-1, keepdims=True))
    y_xa = jnp.concatenate(heads, axis=-1).astype(BF16)
    merged += gc_ref[...].astype(F32) * _dot(y_xa, wc_ref[...])

    o_ref[...] = x_ref[...] + _dot(merged.astype(BF16), wo_ref[...])


def _branch(z, y_fox, x2, mk, mv, cw, cbias, wa, wb, wc, wo, seq):
    n, d = x2.shape
    tm = TM_BR
    nt = n // tm
    tps = seq // tm
    n_mem = mk.shape[1]
    hpt = tm // HALO_ROWS

    def zspec(g):
        return pl.BlockSpec((tm, d), lambda i: (i, g))

    def halo(g):
        return pl.BlockSpec((HALO_ROWS, d), lambda i: (jnp.maximum(i * hpt - 1, 0), g))

    kern = functools.partial(_branch_kernel, tps)
    return pl.pallas_call(
        kern,
        grid=(nt,),
        in_specs=[
            zspec(G_CB), zspec(G_U), halo(G_U), zspec(G_XQ),
            zspec(G_GA), zspec(G_GB), zspec(G_GC),
            pl.BlockSpec((tm, d), lambda i: (i, 0)),
            pl.BlockSpec((tm, d), lambda i: (i, 0)),
            pl.BlockSpec((1, n_mem, d), lambda i: (i // tps, 0, 0)),
            pl.BlockSpec((1, n_mem, d), lambda i: (i // tps, 0, 0)),
            _const_spec((CONV_WIDTH, d)),
            _const_spec((1, d)),
            _const_spec((d, d)), _const_spec((d, d)), _const_spec((d, d)), _const_spec((d, d)),
        ],
        out_specs=pl.BlockSpec((tm, d), lambda i: (i, 0)),
        out_shape=jax.ShapeDtypeStruct((n, d), F32),
        compiler_params=pltpu.CompilerParams(
            dimension_semantics=("arbitrary",), vmem_limit_bytes=VMEM_LIMIT),
        name="branch",
    )(z, z, z, z, z, z, z, y_fox, x2, mk, mv, cw, cbias, wa, wb, wc, wo)


def _ffn_kernel(x_ref, g_ref, wi_ref, wo_ref, o_ref, a_ref):
    x = x_ref[...]
    h = _rms(x, g_ref[...]).astype(BF16)
    for c in range(D_FF // FF_CHUNK):
        g = _dot(h, wi_ref[:, c * FF_CHUNK:(c + 1) * FF_CHUNK])
        u = _dot(h, wi_ref[:, D_FF + c * FF_CHUNK:D_FF + (c + 1) * FF_CHUNK])
        a_ref[:, c * FF_CHUNK:(c + 1) * FF_CHUNK] = (g * jax.nn.sigmoid(g) * u).astype(BF16)
    o_ref[...] = x + _dot(a_ref[...], wo_ref[...])


def _ffn(x1, g, wi, wo):
    n, d = x1.shape
    tm = TM_FFN
    return pl.pallas_call(
        _ffn_kernel,
        grid=(n // tm,),
        in_specs=[
            pl.BlockSpec((tm, d), lambda i: (i, 0)),
            _const_spec((1, d)),
            _const_spec((d, 2 * D_FF)),
            _const_spec((D_FF, d)),
        ],
        out_specs=pl.BlockSpec((tm, d), lambda i: (i, 0)),
        out_shape=jax.ShapeDtypeStruct((n, d), F32),
        scratch_shapes=[pltpu.VMEM((tm, D_FF), BF16)],
        compiler_params=pltpu.CompilerParams(
            dimension_semantics=("arbitrary",), vmem_limit_bytes=VMEM_LIMIT),
        name="ffn",
    )(x1, g, wi, wo)


def kernel(x, mem, norm1_g, w_in, conv_w, conv_b, fox_f_bias, fox_q_g, fox_k_g, mem_norm_g,
           w_mem_kv, xa_q_g, xa_k_g, w_br_conv, w_br_fox, w_br_xa, w_o, norm2_g, w_ffn_in,
           w_ffn_out):
    batch, seq, d = x.shape
    depth = norm1_g.shape[0]
    n_main = N_GROUPS * d
    x2 = x.reshape(batch * seq, d)
    for l in range(depth):
        w_main = w_in[l].astype(BF16)
        w_ff = jnp.pad(w_main[:, n_main:], ((0, 0), (0, LANES - FOX_HEADS)))
        f_bias = jnp.pad(fox_f_bias[l], (0, LANES - FOX_HEADS)).reshape(1, LANES)

        mk, mv = _mem_kv(mem, mem_norm_g[l].reshape(1, d), w_mem_kv[l].astype(BF16),
                         xa_k_g[l].reshape(1, XA_HEAD_DIM))
        z = _in_proj(x2, norm1_g[l].reshape(1, d), w_main, w_ff, f_bias,
                     fox_q_g[l].reshape(1, FOX_HEAD_DIM), fox_k_g[l].reshape(1, FOX_HEAD_DIM),
                     xa_q_g[l].reshape(1, XA_HEAD_DIM), seq)
        y_fox = _fox(z, batch, seq)
        x1 = _branch(z, y_fox, x2, mk, mv, conv_w[l], conv_b[l].reshape(1, d),
                     w_br_conv[l].astype(BF16), w_br_fox[l].astype(BF16),
                     w_br_xa[l].astype(BF16), w_o[l].astype(BF16), seq)
        x2 = _ffn(x1, norm2_g[l].reshape(1, d), w_ffn_in[l].astype(BF16),
                  w_ffn_out[l].astype(BF16))
    return x2.reshape(batch, seq, d)
```

```python
import functools
import math

import jax
import jax.numpy as jnp
from jax import lax
from jax.experimental import pallas as pl
from jax.experimental.pallas import tpu as pltpu

F32 = jnp.float32
BF16 = jnp.bfloat16

D_MODEL = 1024
N_GROUPS = 10
FOX_HEADS = 8
FOX_HEAD_DIM = 128
XA_HEADS = 4
XA_HEAD_DIM = 256
D_FF = 2816
FF_CHUNK = 256
CONV_WIDTH = 3
EPS = 1e-6
LOG2E = math.log2(math.e)
LANES = 128
HALO_ROWS = 16

W_CB, W_CC, W_CV, W_FQ, W_FK, W_FV, W_XQ, W_GA, W_GB, W_GC = range(N_GROUPS)
G_CB, G_U, G_FQ, G_FK, G_FV, G_XQ, G_GA, G_GB, G_GC = range(N_GROUPS - 1)
DEC_COL0 = (N_GROUPS - 1) * D_MODEL
Z_WIDTH = DEC_COL0 + 2 * 128
DEC_K0 = 3 * FOX_HEADS

TM_IN = 512
IN_CHUNK = 256
TQ = 512
TK = 512
FOX_HEADS_PER_STEP = 2
FOX_ROW_CHUNK = 64
TM_BR = 512
TM_FFN = 1024
VMEM_LIMIT = 56 * 1024 * 1024
VMEM_LIMIT_IN_PROJ = 60 * 1024 * 1024


def _rms(x, g):
    ms = jnp.mean(x * x, axis=-1, keepdims=True)
    return x * lax.rsqrt(ms + EPS) * g


def _dot(a, b):
    return jnp.dot(a, b, preferred_element_type=F32)


def _dot_nt(a, b):
    return lax.dot_general(a, b, (((1,), (1,)), ((), ())), preferred_element_type=F32)


def _const_spec(shape):
    nd = len(shape)
    return pl.BlockSpec(shape, lambda *_: (0,) * nd, pipeline_mode=pl.Buffered(1))


def _mem_kv_kernel(mem_ref, g_ref, w_ref, kg_ref, k_ref, v_ref):
    mn = _rms(mem_ref[0], g_ref[...]).astype(BF16)
    kv = _dot(mn, w_ref[...])
    for h in range(XA_HEADS):
        hs = slice(h * XA_HEAD_DIM, (h + 1) * XA_HEAD_DIM)
        k_ref[0, :, hs] = _rms(kv[:, hs], kg_ref[...]).astype(BF16)
    v_ref[0] = kv[:, D_MODEL:].astype(BF16)


def _mem_kv(mem, g, w, kg):
    b, m, d = mem.shape
    return pl.pallas_call(
        _mem_kv_kernel,
        grid=(b,),
        in_specs=[
            pl.BlockSpec((1, m, d), lambda i: (i, 0, 0)),
            _const_spec((1, d)),
            _const_spec((d, 2 * d)),
            _const_spec((1, XA_HEAD_DIM)),
        ],
        out_specs=[
            pl.BlockSpec((1, m, d), lambda i: (i, 0, 0)),
            pl.BlockSpec((1, m, d), lambda i: (i, 0, 0)),
        ],
        out_shape=[jax.ShapeDtypeStruct((b, m, d), BF16)] * 2,
        compiler_params=pltpu.CompilerParams(
            dimension_semantics=("arbitrary",), vmem_limit_bytes=VMEM_LIMIT),
        name="mem_kv",
    )(mem, g, w, kg)


def _log_sigmoid(x):
    return jnp.minimum(x, 0.0) - jnp.log1p(jnp.exp(-jnp.abs(x)))


def _cumsum_rows(x):
    n = x.shape[0]
    row = lax.broadcasted_iota(jnp.int32, x.shape, 0)
    s = 1
    while s < n:
        x = x + jnp.where(row >= s, pltpu.roll(x, s, 0), 0.0)
        s *= 2
    return x


def _in_proj_kernel(tiles_per_seq, x_ref, g1_ref, w_ref, wff_ref, fb_ref, qg_ref, kg_ref,
                    xqg_ref, psel_ref, z_ref, carry_ref, h_ref, cc_ref):
    i = pl.program_id(0)
    tm = x_ref.shape[0]
    d = x_ref.shape[1]
    h_ref[...] = _rms(x_ref[...], g1_ref[...]).astype(BF16)

    lf = _log_sigmoid(_dot(h_ref[...], wff_ref[...]) + fb_ref[...])

    @pl.when(i % tiles_per_seq == 0)
    def _():
        carry_ref[...] = jnp.zeros_like(carry_ref)

    c = _cumsum_rows(lf) + carry_ref[...]
    carry_ref[...] = c[tm - 1:tm, :]
    c = c * LOG2E

    hi = c.astype(BF16)
    mid = (c - hi.astype(F32)).astype(BF16)
    lo = (c - hi.astype(F32) - mid.astype(F32)).astype(BF16)
    dec = _dot(jnp.concatenate([hi, mid, lo], axis=1), psel_ref[...])
    lane = lax.broadcasted_iota(jnp.int32, (tm, LANES), 1)
    q_ones = jnp.where((lane >= DEC_K0) & (lane < 2 * DEC_K0), 1.0, 0.0)
    z_ref[:, DEC_COL0:DEC_COL0 + LANES] = (dec[:, :LANES] + q_ones).astype(BF16)
    z_ref[:, DEC_COL0 + LANES:DEC_COL0 + 2 * LANES] = dec[:, LANES:].astype(BF16)

    def head_norm(r, g_ref, scale, col0):
        hd = g_ref.shape[1]
        for h in range(r.shape[1] // hd):
            z_ref[:, col0 + h * hd:col0 + (h + 1) * hd] = (
                _rms(r[:, h * hd:(h + 1) * hd], g_ref[...]) * scale).astype(BF16)

    n_chunks = N_GROUPS * d // IN_CHUNK
    for ci in range(n_chunks):
        col0 = ci * IN_CHUNK
        wg = col0 // d
        cs = slice(col0 % d, col0 % d + IN_CHUNK)
        zcol0 = col0 if wg == W_CB else col0 - d
        r = _dot(h_ref[...], w_ref[:, col0:col0 + IN_CHUNK])
        if wg == W_CC:
            cc_ref[:, cs] = r
        elif wg == W_CV:
            z_ref[:, zcol0:zcol0 + IN_CHUNK] = (cc_ref[:, cs] * r).astype(BF16)
        elif wg == W_FQ:
            head_norm(r, qg_ref, LOG2E / math.sqrt(FOX_HEAD_DIM), zcol0)
        elif wg == W_FK:
            head_norm(r, kg_ref, 1.0, zcol0)
        elif wg == W_XQ:
            head_norm(r, xqg_ref, LOG2E / math.sqrt(XA_HEAD_DIM), zcol0)
        elif wg in (W_GA, W_GB, W_GC):
            z_ref[:, zcol0:zcol0 + IN_CHUNK] = jax.nn.sigmoid(r).astype(BF16)
        else:
            z_ref[:, zcol0:zcol0 + IN_CHUNK] = r.astype(BF16)


def _decay_selector():
    sel = [[0.0] * (2 * LANES) for _ in range(3 * LANES)]
    for h in range(FOX_HEADS):
        for j in range(3):
            sel[j * LANES + h][3 * h + j] = 1.0
            sel[j * LANES + h][LANES + DEC_K0 + 3 * h + j] = -1.0
    return jnp.array(sel, BF16)


def _in_proj(x2, g1, w, wff, fb, qg, kg, xqg, seq):
    n, d = x2.shape
    tm = TM_IN
    kern = functools.partial(_in_proj_kernel, seq // tm)
    psel = _decay_selector()
    return pl.pallas_call(
        kern,
        grid=(n // tm,),
        in_specs=[
            pl.BlockSpec((tm, d), lambda i: (i, 0)),
            _const_spec((1, d)),
            _const_spec(w.shape),
            _const_spec((d, LANES)),
            _const_spec((1, LANES)),
            _const_spec((1, FOX_HEAD_DIM)),
            _const_spec((1, FOX_HEAD_DIM)),
            _const_spec((1, XA_HEAD_DIM)),
            _const_spec(psel.shape),
        ],
        out_specs=pl.BlockSpec((tm, Z_WIDTH), lambda i: (i, 0)),
        out_shape=jax.ShapeDtypeStruct((n, Z_WIDTH), BF16),
        scratch_shapes=[
            pltpu.VMEM((1, LANES), F32),
            pltpu.VMEM((tm, d), BF16),
            pltpu.VMEM((tm, d), F32),
        ],
        compiler_params=pltpu.CompilerParams(
            dimension_semantics=("arbitrary",), vmem_limit_bytes=VMEM_LIMIT_IN_PROJ),
        name="in_proj",
    )(x2, g1, w, wff, fb, qg, kg, xqg, psel)


def _fox_kernel(q_ref, qx_ref, k_ref, kx_ref, v_ref, o_ref,
                t_ref, p_ref, m_ref, alpha_ref, acc_ref, kxh_ref):
    qi = pl.program_id(1)
    row = lax.broadcasted_iota(jnp.int32, (TQ, TK), 0)
    col = lax.broadcasted_iota(jnp.int32, (TQ, TK), 1)
    causal = col <= row
    ones = jnp.ones((TK, FOX_HEAD_DIM), BF16)

    @pl.when(qi == 0)
    def _():
        lane = lax.broadcasted_iota(jnp.int32, kx_ref.shape, 1)
        kx = kx_ref[...]
        for h in range(FOX_HEADS):
            own_c = (lane >= DEC_K0 + 3 * h) & (lane < DEC_K0 + 3 * h + 3)
            pick_q = (lane >= 3 * h) & (lane < 3 * h + 3)
            kxh_ref[h] = jnp.where(own_c, kx, jnp.where(pick_q, 1.0, 0.0).astype(BF16))

    def row_parts(diagonal):
        if diagonal:
            return ((0, TQ // 2, TK // 2), (TQ // 2, TQ, TK))
        return ((0, TQ, TK),)

    def logits(h, s, kb, diagonal):
        hs = slice(h * FOX_HEAD_DIM, (h + 1) * FOX_HEAD_DIM)
        for r0, r1, width in row_parts(diagonal):
            keys = slice(kb * TK, kb * TK + width)
            t = _dot_nt(jnp.concatenate([q_ref[r0:r1, hs], qx_ref[r0:r1, :]], axis=1),
                        jnp.concatenate([k_ref[keys, hs], kxh_ref[h, keys, :]], axis=1))
            if diagonal:
                t = jnp.where(causal[r0:r1, :width], t, -jnp.inf)
            t_ref[kb % 2, s, r0:r1, :width] = t

    def accumulate(h, s, kb, diagonal):
        hs = slice(h * FOX_HEAD_DIM, (h + 1) * FOX_HEAD_DIM)
        par = kb % 2
        for r0, r1, width in row_parts(diagonal):
            for r in range(r0, r1, FOX_ROW_CHUNK):
                rows = slice(r, r + FOX_ROW_CHUNK)
                t = t_ref[par, s, rows, :width]
                m_new = jnp.broadcast_to(
                    jnp.max(t, axis=-1, keepdims=True), (FOX_ROW_CHUNK, LANES))
                if kb > 0:
                    m_old = m_ref[s, rows, :]
                    m_new = jnp.maximum(m_old, m_new)
                    alpha_ref[s, rows, :] = jnp.exp2(m_old - m_new)
                m_ref[s, rows, :] = m_new
                p_ref[par, s, rows, :width] = jnp.exp2(
                    t - jnp.tile(m_new, (1, width // LANES))).astype(BF16)
            keys = slice(kb * TK, kb * TK + width)
            v_ext = jnp.concatenate([v_ref[keys, hs], ones[:width]], axis=1)
            pv = _dot(p_ref[par, s, r0:r1, :width], v_ext)
            if kb > 0:
                pv += jnp.tile(alpha_ref[s, r0:r1, :], (1, 2)) * acc_ref[s, r0:r1, :]
            acc_ref[s, r0:r1, :] = pv

    def q_tile(n):
        for h0 in range(0, FOX_HEADS, FOX_HEADS_PER_STEP):
            heads = list(enumerate(range(h0, h0 + FOX_HEADS_PER_STEP)))
            for s, h in heads:
                logits(h, s, 0, n == 0)
            for kb in range(n + 1):
                if kb < n:
                    for s, h in heads:
                        logits(h, s, kb + 1, kb + 1 == n)
                for s, h in heads:
                    accumulate(h, s, kb, kb == n)
            for s, h in heads:
                hs = slice(h * FOX_HEAD_DIM, (h + 1) * FOX_HEAD_DIM)
                o_ref[:, hs] = (acc_ref[s, :, :FOX_HEAD_DIM]
                                / acc_ref[s, :, FOX_HEAD_DIM:]).astype(BF16)

    for n in range(k_ref.shape[0] // TQ):
        pl.when(qi == n)(functools.partial(q_tile, n))


def _fox(z, batch, seq):
    n = z.shape[0]
    d = D_MODEL
    nq = seq // TQ
    return pl.pallas_call(
        _fox_kernel,
        grid=(batch, nq),
        in_specs=[
            pl.BlockSpec((TQ, d), lambda b, q: (b * nq + q, G_FQ)),
            pl.BlockSpec((TQ, LANES), lambda b, q: (b * nq + q, DEC_COL0 // LANES)),
            pl.BlockSpec((seq, d), lambda b, q: (b, G_FK)),
            pl.BlockSpec((seq, LANES), lambda b, q: (b, DEC_COL0 // LANES + 1)),
            pl.BlockSpec((seq, d), lambda b, q: (b, G_FV)),
        ],
        out_specs=pl.BlockSpec((TQ, d), lambda b, q: (b * nq + q, 0)),
        out_shape=jax.ShapeDtypeStruct((n, d), BF16),
        scratch_shapes=[
            pltpu.VMEM((2, FOX_HEADS_PER_STEP, TQ, TK), F32),
            pltpu.VMEM((2, FOX_HEADS_PER_STEP, TQ, TK), BF16),
            pltpu.VMEM((FOX_HEADS_PER_STEP, TQ, LANES), F32),
            pltpu.VMEM((FOX_HEADS_PER_STEP, TQ, LANES), F32),
            pltpu.VMEM((FOX_HEADS_PER_STEP, TQ, 2 * FOX_HEAD_DIM), F32),
            pltpu.VMEM((FOX_HEADS, seq, LANES), BF16),
        ],
        compiler_params=pltpu.CompilerParams(
            dimension_semantics=("arbitrary", "arbitrary"), vmem_limit_bytes=VMEM_LIMIT),
        name="fox",
    )(z, z, z, z, z)


def _branch_kernel(tiles_per_seq, cb_ref, u_ref, up_ref, xq_ref,
                   ga_ref, gb_ref, gc_ref, yf_ref, x_ref, mk_ref, mv_ref,
                   cw_ref, cbias_ref, wa_ref, wb_ref, wc_ref, wo_ref, o_ref):
    i = pl.program_id(0)

    u = u_ref[...].astype(F32)
    first = (i % tiles_per_seq) == 0
    prev = jnp.where(first, 0.0, up_ref[...].astype(F32))
    hrow = lax.broadcasted_iota(jnp.int32, (HALO_ROWS, D_MODEL), 0)
    u1 = pltpu.roll(u, 1, 0)
    u2 = pltpu.roll(u, 2, 0)
    u1 = jnp.concatenate(
        [jnp.where(hrow < 1, pltpu.roll(prev, 1, 0), u1[:HALO_ROWS]), u1[HALO_ROWS:]], axis=0)
    u2 = jnp.concatenate(
        [jnp.where(hrow < 2, pltpu.roll(prev, 2, 0), u2[:HALO_ROWS]), u2[HALO_ROWS:]], axis=0)
    conv = cw_ref[0:1, :] * u2 + cw_ref[1:2, :] * u1 + cw_ref[2:3, :] * u
    y_conv = cb_ref[...].astype(F32) * (conv + cbias_ref[...])
    merged = ga_ref[...].astype(F32) * _dot(y_conv.astype(BF16), wa_ref[...])

    merged += gb_ref[...].astype(F32) * _dot(yf_ref[...], wb_ref[...])

    heads = []
    for h in range(XA_HEADS):
        hs = slice(h * XA_HEAD_DIM, (h + 1) * XA_HEAD_DIM)
        s = _dot_nt(xq_ref[:, hs], mk_ref[0, :, hs])
        p = jnp.exp2(s - jnp.max(s, axis=-1, keepdims=True))
        o = _dot(p.astype(BF16), mv_ref[0, :, hs])
        heads.append(o / jnp.sum(p, axis=-1, keepdims=True))
    y_xa = jnp.concatenate(heads, axis=-1).astype(BF16)
    merged += gc_ref[...].astype(F32) * _dot(y_xa, wc_ref[...])

    o_ref[...] = x_ref[...] + _dot(merged.astype(BF16), wo_ref[...])


def _branch(z, y_fox, x2, mk, mv, cw, cbias, wa, wb, wc, wo, seq):
    n, d = x2.shape
    tm = TM_BR
    nt = n // tm
    tps = seq // tm
    n_mem = mk.shape[1]
    hpt = tm // HALO_ROWS

    def zspec(g):
        return pl.BlockSpec((tm, d), lambda i: (i, g))

    def halo(g):
        return pl.BlockSpec((HALO_ROWS, d), lambda i: (jnp.maximum(i * hpt - 1, 0), g))

    kern = functools.partial(_branch_kernel, tps)
    return pl.pallas_call(
        kern,
        grid=(nt,),
        in_specs=[
            zspec(G_CB), zspec(G_U), halo(G_U), zspec(G_XQ),
            zspec(G_GA), zspec(G_GB), zspec(G_GC),
            pl.BlockSpec((tm, d), lambda i: (i, 0)),
            pl.BlockSpec((tm, d), lambda i: (i, 0)),
            pl.BlockSpec((1, n_mem, d), lambda i: (i // tps, 0, 0)),
            pl.BlockSpec((1, n_mem, d), lambda i: (i // tps, 0, 0)),
            _const_spec((CONV_WIDTH, d)),
            _const_spec((1, d)),
            _const_spec((d, d)), _const_spec((d, d)), _const_spec((d, d)), _const_spec((d, d)),
        ],
        out_specs=pl.BlockSpec((tm, d), lambda i: (i, 0)),
        out_shape=jax.ShapeDtypeStruct((n, d), F32),
        compiler_params=pltpu.CompilerParams(
            dimension_semantics=("arbitrary",), vmem_limit_bytes=VMEM_LIMIT),
        name="branch",
    )(z, z, z, z, z, z, z, y_fox, x2, mk, mv, cw, cbias, wa, wb, wc, wo)


def _ffn_kernel(x_ref, g_ref, wi_ref, wo_ref, o_ref, a_ref):
    x = x_ref[...]
    h = _rms(x, g_ref[...]).astype(BF16)
    for c in range(D_FF // FF_CHUNK):
        g = _dot(h, wi_ref[:, c * FF_CHUNK:(c + 1) * FF_CHUNK])
        u = _dot(h, wi_ref[:, D_FF + c * FF_CHUNK:D_FF + (c + 1) * FF_CHUNK])
        a_ref[:, c * FF_CHUNK:(c + 1) * FF_CHUNK] = (g * jax.nn.sigmoid(g) * u).astype(BF16)
    o_ref[...] = x + _dot(a_ref[...], wo_ref[...])


def _ffn(x1, g, wi, wo):
    n, d = x1.shape
    tm = TM_FFN
    return pl.pallas_call(
        _ffn_kernel,
        grid=(n // tm,),
        in_specs=[
            pl.BlockSpec((tm, d), lambda i: (i, 0)),
            _const_spec((1, d)),
            _const_spec((d, 2 * D_FF)),
            _const_spec((D_FF, d)),
        ],
        out_specs=pl.BlockSpec((tm, d), lambda i: (i, 0)),
        out_shape=jax.ShapeDtypeStruct((n, d), F32),
        scratch_shapes=[pltpu.VMEM((tm, D_FF), BF16)],
        compiler_params=pltpu.CompilerParams(
            dimension_semantics=("arbitrary",), vmem_limit_bytes=VMEM_LIMIT),
        name="ffn",
    )(x1, g, wi, wo)


def kernel(x, mem, norm1_g, w_in, conv_w, conv_b, fox_f_bias, fox_q_g, fox_k_g, mem_norm_g,
           w_mem_kv, xa_q_g, xa_k_g, w_br_conv, w_br_fox, w_br_xa, w_o, norm2_g, w_ffn_in,
           w_ffn_out):
    batch, seq, d = x.shape
    depth = norm1_g.shape[0]
    n_main = N_GROUPS * d
    x2 = x.reshape(batch * seq, d)
    for l in range(depth):
        w_main = w_in[l].astype(BF16)
        w_ff = jnp.pad(w_main[:, n_main:], ((0, 0), (0, LANES - FOX_HEADS)))
        f_bias = jnp.pad(fox_f_bias[l], (0, LANES - FOX_HEADS)).reshape(1, LANES)

        mk, mv = _mem_kv(mem, mem_norm_g[l].reshape(1, d), w_mem_kv[l].astype(BF16),
                         xa_k_g[l].reshape(1, XA_HEAD_DIM))
        z = _in_proj(x2, norm1_g[l].reshape(1, d), w_main, w_ff, f_bias,
                     fox_q_g[l].reshape(1, FOX_HEAD_DIM), fox_k_g[l].reshape(1, FOX_HEAD_DIM),
                     xa_q_g[l].reshape(1, XA_HEAD_DIM), seq)
        y_fox = _fox(z, batch, seq)
        x1 = _branch(z, y_fox, x2, mk, mv, conv_w[l], conv_b[l].reshape(1, d),
                     w_br_conv[l].astype(BF16), w_br_fox[l].astype(BF16),
                     w_br_xa[l].astype(BF16), w_o[l].astype(BF16), seq)
        x2 = _ffn(x1, norm2_g[l].reshape(1, d), w_ffn_in[l].astype(BF16),
                  w_ffn_out[l].astype(BF16))
    return x2.reshape(batch, seq, d)
```

```python
import functools
import math

import jax
import jax.numpy as jnp
from jax import lax
from jax.experimental import pallas as pl
from jax.experimental.pallas import tpu as pltpu

F32 = jnp.float32
BF16 = jnp.bfloat16

D_MODEL = 1024
N_GROUPS = 10
FOX_HEADS = 8
FOX_HEAD_DIM = 128
XA_HEADS = 4
XA_HEAD_DIM = 256
D_FF = 2816
FF_CHUNK = 256
CONV_WIDTH = 3
EPS = 1e-6
LOG2E = math.log2(math.e)
LANES = 128
CONV_HALO = 8

W_CB, W_CC, W_CV, W_FQ, W_FK, W_FV, W_XQ, W_GA, W_GB, W_GC = range(N_GROUPS)
G_YC, G_FQ, G_FK, G_FV, G_XQ, G_GA, G_GB, G_GC = range(N_GROUPS - 2)
DEC_COL0 = (N_GROUPS - 2) * D_MODEL
Z_WIDTH = DEC_COL0 + 2 * 128
DEC_K0 = 3 * FOX_HEADS

TM_IN = 512
IN_CHUNK = 256
TQ = 512
TK = 512
FOX_HEADS_PER_STEP = 2
FOX_ROW_CHUNK = 64
TM_BR = 512
TM_FFN = 1024
VMEM_LIMIT = 56 * 1024 * 1024
VMEM_LIMIT_IN_PROJ = 60 * 1024 * 1024


def _rms(x, g):
    ms = jnp.mean(x * x, axis=-1, keepdims=True)
    return x * lax.rsqrt(ms + EPS) * g


def _dot(a, b):
    return jnp.dot(a, b, preferred_element_type=F32)


def _dot_nt(a, b):
    return lax.dot_general(a, b, (((1,), (1,)), ((), ())), preferred_element_type=F32)


def _const_spec(shape):
    nd = len(shape)
    return pl.BlockSpec(shape, lambda *_: (0,) * nd, pipeline_mode=pl.Buffered(1))


def _mem_kv_kernel(mem_ref, g_ref, w_ref, kg_ref, k_ref, v_ref):
    mn = _rms(mem_ref[0], g_ref[...]).astype(BF16)
    kv = _dot(mn, w_ref[...])
    for h in range(XA_HEADS):
        hs = slice(h * XA_HEAD_DIM, (h + 1) * XA_HEAD_DIM)
        k_ref[0, :, hs] = _rms(kv[:, hs], kg_ref[...]).astype(BF16)
    v_ref[0] = kv[:, D_MODEL:].astype(BF16)


def _mem_kv(mem, g, w, kg):
    b, m, d = mem.shape
    return pl.pallas_call(
        _mem_kv_kernel,
        grid=(b,),
        in_specs=[
            pl.BlockSpec((1, m, d), lambda i: (i, 0, 0)),
            _const_spec((1, d)),
            _const_spec((d, 2 * d)),
            _const_spec((1, XA_HEAD_DIM)),
        ],
        out_specs=[
            pl.BlockSpec((1, m, d), lambda i: (i, 0, 0)),
            pl.BlockSpec((1, m, d), lambda i: (i, 0, 0)),
        ],
        out_shape=[jax.ShapeDtypeStruct((b, m, d), BF16)] * 2,
        compiler_params=pltpu.CompilerParams(
            dimension_semantics=("arbitrary",), vmem_limit_bytes=VMEM_LIMIT),
        name="mem_kv",
    )(mem, g, w, kg)


def _log_sigmoid(x):
    return jnp.minimum(x, 0.0) - jnp.log1p(jnp.exp(-jnp.abs(x)))


def _cumsum_rows(x):
    n = x.shape[0]
    row = lax.broadcasted_iota(jnp.int32, x.shape, 0)
    s = 1
    while s < n:
        x = x + jnp.where(row >= s, pltpu.roll(x, s, 0), 0.0)
        s *= 2
    return x


def _in_proj_kernel(tiles_per_seq, x_ref, g1_ref, w_ref, wff_ref, fb_ref, qg_ref, kg_ref,
                    xqg_ref, psel_ref, cw_ref, cbias_ref, z_ref,
                    carry_ref, h_ref, cb_ref, cc_ref, ucarry_ref):
    i = pl.program_id(0)
    tm = x_ref.shape[0]
    d = x_ref.shape[1]
    h_ref[...] = _rms(x_ref[...], g1_ref[...]).astype(BF16)

    lf = _log_sigmoid(_dot(h_ref[...], wff_ref[...]) + fb_ref[...])

    @pl.when(i % tiles_per_seq == 0)
    def _():
        carry_ref[...] = jnp.zeros_like(carry_ref)
        ucarry_ref[...] = jnp.zeros_like(ucarry_ref)

    c = _cumsum_rows(lf) + carry_ref[...]
    carry_ref[...] = c[tm - 1:tm, :]
    c = c * LOG2E

    hi = c.astype(BF16)
    mid = (c - hi.astype(F32)).astype(BF16)
    lo = (c - hi.astype(F32) - mid.astype(F32)).astype(BF16)
    dec = _dot(jnp.concatenate([hi, mid, lo], axis=1), psel_ref[...])
    lane = lax.broadcasted_iota(jnp.int32, (tm, LANES), 1)
    q_ones = jnp.where((lane >= DEC_K0) & (lane < 2 * DEC_K0), 1.0, 0.0)
    z_ref[:, DEC_COL0:DEC_COL0 + LANES] = (dec[:, :LANES] + q_ones).astype(BF16)
    z_ref[:, DEC_COL0 + LANES:DEC_COL0 + 2 * LANES] = dec[:, LANES:].astype(BF16)

    def head_norm(r, g_ref, scale, col0):
        hd = g_ref.shape[1]
        for h in range(r.shape[1] // hd):
            z_ref[:, col0 + h * hd:col0 + (h + 1) * hd] = (
                _rms(r[:, h * hd:(h + 1) * hd], g_ref[...]) * scale).astype(BF16)

    hrow = lax.broadcasted_iota(jnp.int32, (CONV_HALO, IN_CHUNK), 0)

    def gated_conv(cv, cs):
        u = cc_ref[:, cs] * cv
        prev = ucarry_ref[:, cs]
        ucarry_ref[:, cs] = u[tm - CONV_HALO:, :]
        u1 = pltpu.roll(u, 1, 0)
        u2 = pltpu.roll(u, 2, 0)
        u1 = jnp.concatenate(
            [jnp.where(hrow < 1, pltpu.roll(prev, 1, 0), u1[:CONV_HALO]), u1[CONV_HALO:]], axis=0)
        u2 = jnp.concatenate(
            [jnp.where(hrow < 2, pltpu.roll(prev, 2, 0), u2[:CONV_HALO]), u2[CONV_HALO:]], axis=0)
        conv = cw_ref[0:1, cs] * u2 + cw_ref[1:2, cs] * u1 + cw_ref[2:3, cs] * u
        return cb_ref[:, cs] * (conv + cbias_ref[:, cs])

    n_chunks = N_GROUPS * d // IN_CHUNK
    for ci in range(n_chunks):
        col0 = ci * IN_CHUNK
        wg = col0 // d
        cs = slice(col0 % d, col0 % d + IN_CHUNK)
        zcol0 = col0 - (W_CV - G_YC) * d
        r = _dot(h_ref[...], w_ref[:, col0:col0 + IN_CHUNK])
        if wg == W_CB:
            cb_ref[:, cs] = r
        elif wg == W_CC:
            cc_ref[:, cs] = r
        elif wg == W_CV:
            z_ref[:, zcol0:zcol0 + IN_CHUNK] = gated_conv(r, cs).astype(BF16)
        elif wg == W_FQ:
            head_norm(r, qg_ref, LOG2E / math.sqrt(FOX_HEAD_DIM), zcol0)
        elif wg == W_FK:
            head_norm(r, kg_ref, 1.0, zcol0)
        elif wg == W_XQ:
            head_norm(r, xqg_ref, LOG2E / math.sqrt(XA_HEAD_DIM), zcol0)
        elif wg in (W_GA, W_GB, W_GC):
            z_ref[:, zcol0:zcol0 + IN_CHUNK] = jax.nn.sigmoid(r).astype(BF16)
        else:
            z_ref[:, zcol0:zcol0 + IN_CHUNK] = r.astype(BF16)


def _decay_selector():
    sel = [[0.0] * (2 * LANES) for _ in range(3 * LANES)]
    for h in range(FOX_HEADS):
        for j in range(3):
            sel[j * LANES + h][3 * h + j] = 1.0
            sel[j * LANES + h][LANES + DEC_K0 + 3 * h + j] = -1.0
    return jnp.array(sel, BF16)


def _in_proj(x2, g1, w, wff, fb, qg, kg, xqg, cw, cbias, seq):
    n, d = x2.shape
    tm = TM_IN
    kern = functools.partial(_in_proj_kernel, seq // tm)
    psel = _decay_selector()
    return pl.pallas_call(
        kern,
        grid=(n // tm,),
        in_specs=[
            pl.BlockSpec((tm, d), lambda i: (i, 0)),
            _const_spec((1, d)),
            _const_spec(w.shape),
            _const_spec((d, LANES)),
            _const_spec((1, LANES)),
            _const_spec((1, FOX_HEAD_DIM)),
            _const_spec((1, FOX_HEAD_DIM)),
            _const_spec((1, XA_HEAD_DIM)),
            _const_spec(psel.shape),
            _const_spec((CONV_WIDTH, d)),
            _const_spec((1, d)),
        ],
        out_specs=pl.BlockSpec((tm, Z_WIDTH), lambda i: (i, 0)),
        out_shape=jax.ShapeDtypeStruct((n, Z_WIDTH), BF16),
        scratch_shapes=[
            pltpu.VMEM((1, LANES), F32),
            pltpu.VMEM((tm, d), BF16),
            pltpu.VMEM((tm, d), F32),
            pltpu.VMEM((tm, d), F32),
            pltpu.VMEM((CONV_HALO, d), F32),
        ],
        compiler_params=pltpu.CompilerParams(
            dimension_semantics=("arbitrary",), vmem_limit_bytes=VMEM_LIMIT_IN_PROJ),
        name="in_proj",
    )(x2, g1, w, wff, fb, qg, kg, xqg, psel, cw, cbias)


def _fox_kernel(q_ref, qx_ref, k_ref, kx_ref, v_ref, o_ref,
                t_ref, p_ref, m_ref, alpha_ref, acc_ref, kxh_ref):
    qi = pl.program_id(1)
    row = lax.broadcasted_iota(jnp.int32, (TQ, TK), 0)
    col = lax.broadcasted_iota(jnp.int32, (TQ, TK), 1)
    causal = col <= row
    ones = jnp.ones((TK, FOX_HEAD_DIM), BF16)

    @pl.when(qi == 0)
    def _():
        lane = lax.broadcasted_iota(jnp.int32, kx_ref.shape, 1)
        kx = kx_ref[...]
        for h in range(FOX_HEADS):
            own_c = (lane >= DEC_K0 + 3 * h) & (lane < DEC_K0 + 3 * h + 3)
            pick_q = (lane >= 3 * h) & (lane < 3 * h + 3)
            kxh_ref[h] = jnp.where(own_c, kx, jnp.where(pick_q, 1.0, 0.0).astype(BF16))

    def row_parts(diagonal):
        if diagonal:
            return ((0, TQ // 2, TK // 2), (TQ // 2, TQ, TK))
        return ((0, TQ, TK),)

    def logits(h, s, kb, diagonal):
        hs = slice(h * FOX_HEAD_DIM, (h + 1) * FOX_HEAD_DIM)
        for r0, r1, width in row_parts(diagonal):
            keys = slice(kb * TK, kb * TK + width)
            t = _dot_nt(jnp.concatenate([q_ref[r0:r1, hs], qx_ref[r0:r1, :]], axis=1),
                        jnp.concatenate([k_ref[keys, hs], kxh_ref[h, keys, :]], axis=1))
            if diagonal:
                t = jnp.where(causal[r0:r1, :width], t, -jnp.inf)
            t_ref[kb % 2, s, r0:r1, :width] = t

    def accumulate(h, s, kb, diagonal):
        hs = slice(h * FOX_HEAD_DIM, (h + 1) * FOX_HEAD_DIM)
        par = kb % 2
        for r0, r1, width in row_parts(diagonal):
            for r in range(r0, r1, FOX_ROW_CHUNK):
                rows = slice(r, r + FOX_ROW_CHUNK)
                t = t_ref[par, s, rows, :width]
                m_new = jnp.broadcast_to(
                    jnp.max(t, axis=-1, keepdims=True), (FOX_ROW_CHUNK, LANES))
                if kb > 0:
                    m_old = m_ref[s, rows, :]
                    m_new = jnp.maximum(m_old, m_new)
                    alpha_ref[s, rows, :] = jnp.exp2(m_old - m_new)
                m_ref[s, rows, :] = m_new
                p_ref[par, s, rows, :width] = jnp.exp2(
                    t - jnp.tile(m_new, (1, width // LANES))).astype(BF16)
            keys = slice(kb * TK, kb * TK + width)
            v_ext = jnp.concatenate([v_ref[keys, hs], ones[:width]], axis=1)
            pv = _dot(p_ref[par, s, r0:r1, :width], v_ext)
            if kb > 0:
                pv += jnp.tile(alpha_ref[s, r0:r1, :], (1, 2)) * acc_ref[s, r0:r1, :]
            acc_ref[s, r0:r1, :] = pv

    def q_tile(n):
        for h0 in range(0, FOX_HEADS, FOX_HEADS_PER_STEP):
            heads = list(enumerate(range(h0, h0 + FOX_HEADS_PER_STEP)))
            for s, h in heads:
                logits(h, s, 0, n == 0)
            for kb in range(n + 1):
                if kb < n:
                    for s, h in heads:
                        logits(h, s, kb + 1, kb + 1 == n)
                for s, h in heads:
                    accumulate(h, s, kb, kb == n)
            for s, h in heads:
                hs = slice(h * FOX_HEAD_DIM, (h + 1) * FOX_HEAD_DIM)
                o_ref[:, hs] = (acc_ref[s, :, :FOX_HEAD_DIM]
                                / acc_ref[s, :, FOX_HEAD_DIM:]).astype(BF16)

    for n in range(k_ref.shape[0] // TQ):
        pl.when(qi == n)(functools.partial(q_tile, n))


def _fox(z, batch, seq):
    n = z.shape[0]
    d = D_MODEL
    nq = seq // TQ
    return pl.pallas_call(
        _fox_kernel,
        grid=(batch, nq),
        in_specs=[
            pl.BlockSpec((TQ, d), lambda b, q: (b * nq + q, G_FQ)),
            pl.BlockSpec((TQ, LANES), lambda b, q: (b * nq + q, DEC_COL0 // LANES)),
            pl.BlockSpec((seq, d), lambda b, q: (b, G_FK)),
            pl.BlockSpec((seq, LANES), lambda b, q: (b, DEC_COL0 // LANES + 1)),
            pl.BlockSpec((seq, d), lambda b, q: (b, G_FV)),
        ],
        out_specs=pl.BlockSpec((TQ, d), lambda b, q: (b * nq + q, 0)),
        out_shape=jax.ShapeDtypeStruct((n, d), BF16),
        scratch_shapes=[
            pltpu.VMEM((2, FOX_HEADS_PER_STEP, TQ, TK), F32),
            pltpu.VMEM((2, FOX_HEADS_PER_STEP, TQ, TK), BF16),
            pltpu.VMEM((FOX_HEADS_PER_STEP, TQ, LANES), F32),
            pltpu.VMEM((FOX_HEADS_PER_STEP, TQ, LANES), F32),
            pltpu.VMEM((FOX_HEADS_PER_STEP, TQ, 2 * FOX_HEAD_DIM), F32),
            pltpu.VMEM((FOX_HEADS, seq, LANES), BF16),
        ],
        compiler_params=pltpu.CompilerParams(
            dimension_semantics=("arbitrary", "arbitrary"), vmem_limit_bytes=VMEM_LIMIT),
        name="fox",
    )(z, z, z, z, z)


def _branch_kernel(yc_ref, xq_ref, ga_ref, gb_ref, gc_ref, yf_ref, x_ref, mk_ref, mv_ref,
                   wa_ref, wb_ref, wc_ref, wo_ref, o_ref):
    merged = ga_ref[...].astype(F32) * _dot(yc_ref[...], wa_ref[...])

    merged += gb_ref[...].astype(F32) * _dot(yf_ref[...], wb_ref[...])

    heads = []
    for h in range(XA_HEADS):
        hs = slice(h * XA_HEAD_DIM, (h + 1) * XA_HEAD_DIM)
        s = _dot_nt(xq_ref[:, hs], mk_ref[0, :, hs])
        p = jnp.exp2(s - jnp.max(s, axis=-1, keepdims=True))
        o = _dot(p.astype(BF16), mv_ref[0, :, hs])
        heads.append(o / jnp.sum(p, axis=-1, keepdims=True))
    y_xa = jnp.concatenate(heads, axis=-1).astype(BF16)
    merged += gc_ref[...].astype(F32) * _dot(y_xa, wc_ref[...])

    o_ref[...] = x_ref[...] + _dot(merged.astype(BF16), wo_ref[...])


def _branch(z, y_fox, x2, mk, mv, wa, wb, wc, wo, seq):
    n, d = x2.shape
    tm = TM_BR
    nt = n // tm
    tps = seq // tm
    n_mem = mk.shape[1]

    def zspec(g):
        return pl.BlockSpec((tm, d), lambda i: (i, g))

    return pl.pallas_call(
        _branch_kernel,
        grid=(nt,),
        in_specs=[
            zspec(G_YC), zspec(G_XQ), zspec(G_GA), zspec(G_GB), zspec(G_GC),
            pl.BlockSpec((tm, d), lambda i: (i, 0)),
            pl.BlockSpec((tm, d), lambda i: (i, 0)),
            pl.BlockSpec((1, n_mem, d), lambda i: (i // tps, 0, 0)),
            pl.BlockSpec((1, n_mem, d), lambda i: (i // tps, 0, 0)),
            _const_spec((d, d)), _const_spec((d, d)), _const_spec((d, d)), _const_spec((d, d)),
        ],
        out_specs=pl.BlockSpec((tm, d), lambda i: (i, 0)),
        out_shape=jax.ShapeDtypeStruct((n, d), F32),
        compiler_params=pltpu.CompilerParams(
            dimension_semantics=("arbitrary",), vmem_limit_bytes=VMEM_LIMIT),
        name="branch",
    )(z, z, z, z, z, y_fox, x2, mk, mv, wa, wb, wc, wo)


def _ffn_kernel(x_ref, g_ref, wi_ref, wo_ref, o_ref, a_ref):
    x = x_ref[...]
    h = _rms(x, g_ref[...]).astype(BF16)
    for c in range(D_FF // FF_CHUNK):
        g = _dot(h, wi_ref[:, c * FF_CHUNK:(c + 1) * FF_CHUNK])
        u = _dot(h, wi_ref[:, D_FF + c * FF_CHUNK:D_FF + (c + 1) * FF_CHUNK])
        a_ref[:, c * FF_CHUNK:(c + 1) * FF_CHUNK] = (g * jax.nn.sigmoid(g) * u).astype(BF16)
    o_ref[...] = x + _dot(a_ref[...], wo_ref[...])


def _ffn(x1, g, wi, wo):
    n, d = x1.shape
    tm = TM_FFN
    return pl.pallas_call(
        _ffn_kernel,
        grid=(n // tm,),
        in_specs=[
            pl.BlockSpec((tm, d), lambda i: (i, 0)),
            _const_spec((1, d)),
            _const_spec((d, 2 * D_FF)),
            _const_spec((D_FF, d)),
        ],
        out_specs=pl.BlockSpec((tm, d), lambda i: (i, 0)),
        out_shape=jax.ShapeDtypeStruct((n, d), F32),
        scratch_shapes=[pltpu.VMEM((tm, D_FF), BF16)],
        compiler_params=pltpu.CompilerParams(
            dimension_semantics=("arbitrary",), vmem_limit_bytes=VMEM_LIMIT),
        name="ffn",
    )(x1, g, wi, wo)


def kernel(x, mem, norm1_g, w_in, conv_w, conv_b, fox_f_bias, fox_q_g, fox_k_g, mem_norm_g,
           w_mem_kv, xa_q_g, xa_k_g, w_br_conv, w_br_fox, w_br_xa, w_o, norm2_g, w_ffn_in,
           w_ffn_out):
    batch, seq, d = x.shape
    depth = norm1_g.shape[0]
    n_main = N_GROUPS * d
    x2 = x.reshape(batch * seq, d)
    for l in range(depth):
        w_main = w_in[l].astype(BF16)
        w_ff = jnp.pad(w_main[:, n_main:], ((0, 0), (0, LANES - FOX_HEADS)))
        f_bias = jnp.pad(fox_f_bias[l], (0, LANES - FOX_HEADS)).reshape(1, LANES)

        mk, mv = _mem_kv(mem, mem_norm_g[l].reshape(1, d), w_mem_kv[l].astype(BF16),
                         xa_k_g[l].reshape(1, XA_HEAD_DIM))
        z = _in_proj(x2, norm1_g[l].reshape(1, d), w_main, w_ff, f_bias,
                     fox_q_g[l].reshape(1, FOX_HEAD_DIM), fox_k_g[l].reshape(1, FOX_HEAD_DIM),
                     xa_q_g[l].reshape(1, XA_HEAD_DIM), conv_w[l], conv_b[l].reshape(1, d), seq)
        y_fox = _fox(z, batch, seq)
        x1 = _branch(z, y_fox, x2, mk, mv,
                     w_br_conv[l].astype(BF16), w_br_fox[l].astype(BF16),
                     w_br_xa[l].astype(BF16), w_o[l].astype(BF16), seq)
        x2 = _ffn(x1, norm2_g[l].reshape(1, d), w_ffn_in[l].astype(BF16),
                  w_ffn_out[l].astype(BF16))
    return x2.reshape(batch, seq, d)
```

```python
import functools
import math

import jax
import jax.numpy as jnp
from jax import lax
from jax.experimental import pallas as pl
from jax.experimental.pallas import tpu as pltpu

F32 = jnp.float32
BF16 = jnp.bfloat16

D_MODEL = 1024
N_GROUPS = 10
FOX_HEADS = 8
FOX_HEAD_DIM = 128
XA_HEADS = 4
XA_HEAD_DIM = 256
D_FF = 2816
FF_CHUNK = 256
CONV_WIDTH = 3
EPS = 1e-6
LOG2E = math.log2(math.e)
LANES = 128
CONV_HALO = 8

W_CB, W_CC, W_CV, W_FQ, W_FK, W_FV, W_XQ, W_GA, W_GB, W_GC = range(N_GROUPS)
G_YC, G_FQ, G_FK, G_FV, G_XQ, G_GA, G_GB, G_GC = range(N_GROUPS - 2)
DEC_COL0 = (N_GROUPS - 2) * D_MODEL
Z_WIDTH = DEC_COL0 + 2 * 128
DEC_K0 = 3 * FOX_HEADS

TM_IN = 512
IN_CHUNK = 256
W_STAGE_COLS = 512
TQ = 512
TK = 512
FOX_HEADS_PER_STEP = 2
FOX_ROW_CHUNK = 64
TM_BR = 512
TM_FFN = 1024
VMEM_LIMIT = 56 * 1024 * 1024
VMEM_LIMIT_IN_PROJ = 60 * 1024 * 1024


def _rms(x, g):
    ms = jnp.mean(x * x, axis=-1, keepdims=True)
    return x * lax.rsqrt(ms + EPS) * g


def _dot(a, b):
    return jnp.dot(a, b, preferred_element_type=F32)


def _dot_nt(a, b):
    return lax.dot_general(a, b, (((1,), (1,)), ((), ())), preferred_element_type=F32)


def _const_spec(shape):
    nd = len(shape)
    return pl.BlockSpec(shape, lambda *_: (0,) * nd, pipeline_mode=pl.Buffered(1))


def _mem_kv_kernel(mem_ref, g_ref, w_ref, kg_ref, k_ref, v_ref):
    mn = _rms(mem_ref[0], g_ref[...]).astype(BF16)
    kv = _dot(mn, w_ref[...])
    for h in range(XA_HEADS):
        hs = slice(h * XA_HEAD_DIM, (h + 1) * XA_HEAD_DIM)
        k_ref[0, :, hs] = _rms(kv[:, hs], kg_ref[...]).astype(BF16)
    v_ref[0] = kv[:, D_MODEL:].astype(BF16)


def _mem_kv(mem, g, w, kg):
    b, m, d = mem.shape
    return pl.pallas_call(
        _mem_kv_kernel,
        grid=(b,),
        in_specs=[
            pl.BlockSpec((1, m, d), lambda i: (i, 0, 0)),
            _const_spec((1, d)),
            _const_spec((d, 2 * d)),
            _const_spec((1, XA_HEAD_DIM)),
        ],
        out_specs=[
            pl.BlockSpec((1, m, d), lambda i: (i, 0, 0)),
            pl.BlockSpec((1, m, d), lambda i: (i, 0, 0)),
        ],
        out_shape=[jax.ShapeDtypeStruct((b, m, d), BF16)] * 2,
        compiler_params=pltpu.CompilerParams(
            dimension_semantics=("arbitrary",), vmem_limit_bytes=VMEM_LIMIT),
        name="mem_kv",
    )(mem, g, w, kg)


def _log_sigmoid(x):
    return jnp.minimum(x, 0.0) - jnp.log1p(jnp.exp(-jnp.abs(x)))


def _cumsum_rows(x):
    n = x.shape[0]
    row = lax.broadcasted_iota(jnp.int32, x.shape, 0)
    s = 1
    while s < n:
        x = x + jnp.where(row >= s, pltpu.roll(x, s, 0), 0.0)
        s *= 2
    return x


def _load_cast_weight(w_hbm, w_ref, stage_ref, sem_ref):
    n_chunks = w_ref.shape[1] // W_STAGE_COLS

    def chunk_copy(c):
        return pltpu.make_async_copy(
            w_hbm.at[:, pl.ds(c * W_STAGE_COLS, W_STAGE_COLS)],
            stage_ref.at[c % 2], sem_ref.at[c % 2])

    chunk_copy(0).start()
    for c in range(n_chunks):
        if c + 1 < n_chunks:
            chunk_copy(c + 1).start()
        chunk_copy(c).wait()
        w_ref[:, c * W_STAGE_COLS:(c + 1) * W_STAGE_COLS] = stage_ref[c % 2].astype(BF16)


def _in_proj_kernel(tiles_per_seq, x_ref, g1_ref, w_hbm, wff_ref, fb_ref, qg_ref, kg_ref,
                    xqg_ref, psel_ref, cw_ref, cbias_ref, z_ref,
                    carry_ref, h_ref, cb_ref, cc_ref, ucarry_ref, w_ref, stage_ref, sem_ref):
    i = pl.program_id(0)
    tm = x_ref.shape[0]
    d = x_ref.shape[1]

    @pl.when(i == 0)
    def _():
        _load_cast_weight(w_hbm, w_ref, stage_ref, sem_ref)

    h_ref[...] = _rms(x_ref[...], g1_ref[...]).astype(BF16)

    lf = _log_sigmoid(_dot(h_ref[...], wff_ref[...]) + fb_ref[...])

    @pl.when(i % tiles_per_seq == 0)
    def _():
        carry_ref[...] = jnp.zeros_like(carry_ref)
        ucarry_ref[...] = jnp.zeros_like(ucarry_ref)

    c = _cumsum_rows(lf) + carry_ref[...]
    carry_ref[...] = c[tm - 1:tm, :]
    c = c * LOG2E

    hi = c.astype(BF16)
    mid = (c - hi.astype(F32)).astype(BF16)
    lo = (c - hi.astype(F32) - mid.astype(F32)).astype(BF16)
    dec = _dot(jnp.concatenate([hi, mid, lo], axis=1), psel_ref[...])
    lane = lax.broadcasted_iota(jnp.int32, (tm, LANES), 1)
    q_ones = jnp.where((lane >= DEC_K0) & (lane < 2 * DEC_K0), 1.0, 0.0)
    z_ref[:, DEC_COL0:DEC_COL0 + LANES] = (dec[:, :LANES] + q_ones).astype(BF16)
    z_ref[:, DEC_COL0 + LANES:DEC_COL0 + 2 * LANES] = dec[:, LANES:].astype(BF16)

    def head_norm(r, g_ref, scale, col0):
        hd = g_ref.shape[1]
        for h in range(r.shape[1] // hd):
            z_ref[:, col0 + h * hd:col0 + (h + 1) * hd] = (
                _rms(r[:, h * hd:(h + 1) * hd], g_ref[...]) * scale).astype(BF16)

    hrow = lax.broadcasted_iota(jnp.int32, (CONV_HALO, IN_CHUNK), 0)

    def gated_conv(cv, cs):
        u = cc_ref[:, cs] * cv
        prev = ucarry_ref[:, cs]
        ucarry_ref[:, cs] = u[tm - CONV_HALO:, :]
        u1 = pltpu.roll(u, 1, 0)
        u2 = pltpu.roll(u, 2, 0)
        u1 = jnp.concatenate(
            [jnp.where(hrow < 1, pltpu.roll(prev, 1, 0), u1[:CONV_HALO]), u1[CONV_HALO:]], axis=0)
        u2 = jnp.concatenate(
            [jnp.where(hrow < 2, pltpu.roll(prev, 2, 0), u2[:CONV_HALO]), u2[CONV_HALO:]], axis=0)
        conv = cw_ref[0:1, cs] * u2 + cw_ref[1:2, cs] * u1 + cw_ref[2:3, cs] * u
        return cb_ref[:, cs] * (conv + cbias_ref[:, cs])

    n_chunks = N_GROUPS * d // IN_CHUNK
    for ci in range(n_chunks):
        col0 = ci * IN_CHUNK
        wg = col0 // d
        cs = slice(col0 % d, col0 % d + IN_CHUNK)
        zcol0 = col0 - (W_CV - G_YC) * d
        r = _dot(h_ref[...], w_ref[:, col0:col0 + IN_CHUNK])
        if wg == W_CB:
            cb_ref[:, cs] = r
        elif wg == W_CC:
            cc_ref[:, cs] = r
        elif wg == W_CV:
            z_ref[:, zcol0:zcol0 + IN_CHUNK] = gated_conv(r, cs).astype(BF16)
        elif wg == W_FQ:
            head_norm(r, qg_ref, LOG2E / math.sqrt(FOX_HEAD_DIM), zcol0)
        elif wg == W_FK:
            head_norm(r, kg_ref, 1.0, zcol0)
        elif wg == W_XQ:
            head_norm(r, xqg_ref, LOG2E / math.sqrt(XA_HEAD_DIM), zcol0)
        elif wg in (W_GA, W_GB, W_GC):
            z_ref[:, zcol0:zcol0 + IN_CHUNK] = jax.nn.sigmoid(r).astype(BF16)
        else:
            z_ref[:, zcol0:zcol0 + IN_CHUNK] = r.astype(BF16)


def _decay_selector():
    sel = [[0.0] * (2 * LANES) for _ in range(3 * LANES)]
    for h in range(FOX_HEADS):
        for j in range(3):
            sel[j * LANES + h][3 * h + j] = 1.0
            sel[j * LANES + h][LANES + DEC_K0 + 3 * h + j] = -1.0
    return jnp.array(sel, BF16)


def _in_proj(x2, g1, w, wff, fb, qg, kg, xqg, cw, cbias, seq):
    n, d = x2.shape
    tm = TM_IN
    kern = functools.partial(_in_proj_kernel, seq // tm)
    psel = _decay_selector()
    return pl.pallas_call(
        kern,
        grid=(n // tm,),
        in_specs=[
            pl.BlockSpec((tm, d), lambda i: (i, 0)),
            _const_spec((1, d)),
            pl.BlockSpec(memory_space=pl.ANY),
            _const_spec((d, LANES)),
            _const_spec((1, LANES)),
            _const_spec((1, FOX_HEAD_DIM)),
            _const_spec((1, FOX_HEAD_DIM)),
            _const_spec((1, XA_HEAD_DIM)),
            _const_spec(psel.shape),
            _const_spec((CONV_WIDTH, d)),
            _const_spec((1, d)),
        ],
        out_specs=pl.BlockSpec((tm, Z_WIDTH), lambda i: (i, 0)),
        out_shape=jax.ShapeDtypeStruct((n, Z_WIDTH), BF16),
        scratch_shapes=[
            pltpu.VMEM((1, LANES), F32),
            pltpu.VMEM((tm, d), BF16),
            pltpu.VMEM((tm, d), F32),
            pltpu.VMEM((tm, d), F32),
            pltpu.VMEM((CONV_HALO, d), F32),
            pltpu.VMEM((d, N_GROUPS * d), BF16),
            pltpu.VMEM((2, d, W_STAGE_COLS), F32),
            pltpu.SemaphoreType.DMA((2,)),
        ],
        compiler_params=pltpu.CompilerParams(
            dimension_semantics=("arbitrary",), vmem_limit_bytes=VMEM_LIMIT_IN_PROJ),
        name="in_proj",
    )(x2, g1, w, wff, fb, qg, kg, xqg, psel, cw, cbias)


def _fox_kernel(q_ref, qx_ref, k_ref, kx_ref, v_ref, o_ref,
                t_ref, p_ref, m_ref, alpha_ref, acc_ref, kxh_ref):
    qi = pl.program_id(1)
    row = lax.broadcasted_iota(jnp.int32, (TQ, TK), 0)
    col = lax.broadcasted_iota(jnp.int32, (TQ, TK), 1)
    causal = col <= row
    ones = jnp.ones((TK, FOX_HEAD_DIM), BF16)

    @pl.when(qi == 0)
    def _():
        lane = lax.broadcasted_iota(jnp.int32, kx_ref.shape, 1)
        kx = kx_ref[...]
        for h in range(FOX_HEADS):
            own_c = (lane >= DEC_K0 + 3 * h) & (lane < DEC_K0 + 3 * h + 3)
            pick_q = (lane >= 3 * h) & (lane < 3 * h + 3)
            kxh_ref[h] = jnp.where(own_c, kx, jnp.where(pick_q, 1.0, 0.0).astype(BF16))

    def row_parts(diagonal):
        if diagonal:
            return ((0, TQ // 2, TK // 2), (TQ // 2, TQ, TK))
        return ((0, TQ, TK),)

    def logits(h, s, kb, diagonal):
        hs = slice(h * FOX_HEAD_DIM, (h + 1) * FOX_HEAD_DIM)
        for r0, r1, width in row_parts(diagonal):
            keys = slice(kb * TK, kb * TK + width)
            t = _dot_nt(jnp.concatenate([q_ref[r0:r1, hs], qx_ref[r0:r1, :]], axis=1),
                        jnp.concatenate([k_ref[keys, hs], kxh_ref[h, keys, :]], axis=1))
            if diagonal:
                t = jnp.where(causal[r0:r1, :width], t, -jnp.inf)
            t_ref[kb % 2, s, r0:r1, :width] = t

    def accumulate(h, s, kb, diagonal):
        hs = slice(h * FOX_HEAD_DIM, (h + 1) * FOX_HEAD_DIM)
        par = kb % 2
        for r0, r1, width in row_parts(diagonal):
            for r in range(r0, r1, FOX_ROW_CHUNK):
                rows = slice(r, r + FOX_ROW_CHUNK)
                t = t_ref[par, s, rows, :width]
                m_new = jnp.broadcast_to(
                    jnp.max(t, axis=-1, keepdims=True), (FOX_ROW_CHUNK, LANES))
                if kb > 0:
                    m_old = m_ref[s, rows, :]
                    m_new = jnp.maximum(m_old, m_new)
                    alpha_ref[s, rows, :] = jnp.exp2(m_old - m_new)
                m_ref[s, rows, :] = m_new
                p_ref[par, s, rows, :width] = jnp.exp2(
                    t - jnp.tile(m_new, (1, width // LANES))).astype(BF16)
            keys = slice(kb * TK, kb * TK + width)
            v_ext = jnp.concatenate([v_ref[keys, hs], ones[:width]], axis=1)
            pv = _dot(p_ref[par, s, r0:r1, :width], v_ext)
            if kb > 0:
                pv += jnp.tile(alpha_ref[s, r0:r1, :], (1, 2)) * acc_ref[s, r0:r1, :]
            acc_ref[s, r0:r1, :] = pv

    def q_tile(n):
        for h0 in range(0, FOX_HEADS, FOX_HEADS_PER_STEP):
            heads = list(enumerate(range(h0, h0 + FOX_HEADS_PER_STEP)))
            for s, h in heads:
                logits(h, s, 0, n == 0)
            for kb in range(n + 1):
                if kb < n:
                    for s, h in heads:
                        logits(h, s, kb + 1, kb + 1 == n)
                for s, h in heads:
                    accumulate(h, s, kb, kb == n)
            for s, h in heads:
                hs = slice(h * FOX_HEAD_DIM, (h + 1) * FOX_HEAD_DIM)
                o_ref[:, hs] = (acc_ref[s, :, :FOX_HEAD_DIM]
                                / acc_ref[s, :, FOX_HEAD_DIM:]).astype(BF16)

    for n in range(k_ref.shape[0] // TQ):
        pl.when(qi == n)(functools.partial(q_tile, n))


def _fox(z, batch, seq):
    n = z.shape[0]
    d = D_MODEL
    nq = seq // TQ
    return pl.pallas_call(
        _fox_kernel,
        grid=(batch, nq),
        in_specs=[
            pl.BlockSpec((TQ, d), lambda b, q: (b * nq + q, G_FQ)),
            pl.BlockSpec((TQ, LANES), lambda b, q: (b * nq + q, DEC_COL0 // LANES)),
            pl.BlockSpec((seq, d), lambda b, q: (b, G_FK)),
            pl.BlockSpec((seq, LANES), lambda b, q: (b, DEC_COL0 // LANES + 1)),
            pl.BlockSpec((seq, d), lambda b, q: (b, G_FV)),
        ],
        out_specs=pl.BlockSpec((TQ, d), lambda b, q: (b * nq + q, 0)),
        out_shape=jax.ShapeDtypeStruct((n, d), BF16),
        scratch_shapes=[
            pltpu.VMEM((2, FOX_HEADS_PER_STEP, TQ, TK), F32),
            pltpu.VMEM((2, FOX_HEADS_PER_STEP, TQ, TK), BF16),
            pltpu.VMEM((FOX_HEADS_PER_STEP, TQ, LANES), F32),
            pltpu.VMEM((FOX_HEADS_PER_STEP, TQ, LANES), F32),
            pltpu.VMEM((FOX_HEADS_PER_STEP, TQ, 2 * FOX_HEAD_DIM), F32),
            pltpu.VMEM((FOX_HEADS, seq, LANES), BF16),
        ],
        compiler_params=pltpu.CompilerParams(
            dimension_semantics=("arbitrary", "arbitrary"), vmem_limit_bytes=VMEM_LIMIT),
        name="fox",
    )(z, z, z, z, z)


def _branch_kernel(yc_ref, xq_ref, ga_ref, gb_ref, gc_ref, yf_ref, x_ref, mk_ref, mv_ref,
                   wa_ref, wb_ref, wc_ref, wo_ref, o_ref):
    merged = ga_ref[...].astype(F32) * _dot(yc_ref[...], wa_ref[...])

    merged += gb_ref[...].astype(F32) * _dot(yf_ref[...], wb_ref[...])

    heads = []
    for h in range(XA_HEADS):
        hs = slice(h * XA_HEAD_DIM, (h + 1) * XA_HEAD_DIM)
        s = _dot_nt(xq_ref[:, hs], mk_ref[0, :, hs])
        p = jnp.exp2(s - jnp.max(s, axis=-1, keepdims=True))
        o = _dot(p.astype(BF16), mv_ref[0, :, hs])
        heads.append(o / jnp.sum(p, axis=-1, keepdims=True))
    y_xa = jnp.concatenate(heads, axis=-1).astype(BF16)
    merged += gc_ref[...].astype(F32) * _dot(y_xa, wc_ref[...])

    o_ref[...] = x_ref[...] + _dot(merged.astype(BF16), wo_ref[...])


def _branch(z, y_fox, x2, mk, mv, wa, wb, wc, wo, seq):
    n, d = x2.shape
    tm = TM_BR
    nt = n // tm
    tps = seq // tm
    n_mem = mk.shape[1]

    def zspec(g):
        return pl.BlockSpec((tm, d), lambda i: (i, g))

    return pl.pallas_call(
        _branch_kernel,
        grid=(nt,),
        in_specs=[
            zspec(G_YC), zspec(G_XQ), zspec(G_GA), zspec(G_GB), zspec(G_GC),
            pl.BlockSpec((tm, d), lambda i: (i, 0)),
            pl.BlockSpec((tm, d), lambda i: (i, 0)),
            pl.BlockSpec((1, n_mem, d), lambda i: (i // tps, 0, 0)),
            pl.BlockSpec((1, n_mem, d), lambda i: (i // tps, 0, 0)),
            _const_spec((d, d)), _const_spec((d, d)), _const_spec((d, d)), _const_spec((d, d)),
        ],
        out_specs=pl.BlockSpec((tm, d), lambda i: (i, 0)),
        out_shape=jax.ShapeDtypeStruct((n, d), F32),
        compiler_params=pltpu.CompilerParams(
            dimension_semantics=("arbitrary",), vmem_limit_bytes=VMEM_LIMIT),
        name="branch",
    )(z, z, z, z, z, y_fox, x2, mk, mv, wa, wb, wc, wo)


def _ffn_kernel(x_ref, g_ref, wi_ref, wo_ref, o_ref, a_ref):
    x = x_ref[...]
    h = _rms(x, g_ref[...]).astype(BF16)
    for c in range(D_FF // FF_CHUNK):
        g = _dot(h, wi_ref[:, c * FF_CHUNK:(c + 1) * FF_CHUNK])
        u = _dot(h, wi_ref[:, D_FF + c * FF_CHUNK:D_FF + (c + 1) * FF_CHUNK])
        a_ref[:, c * FF_CHUNK:(c + 1) * FF_CHUNK] = (g * jax.nn.sigmoid(g) * u).astype(BF16)
    o_ref[...] = x + _dot(a_ref[...], wo_ref[...])


def _ffn(x1, g, wi, wo):
    n, d = x1.shape
    tm = TM_FFN
    return pl.pallas_call(
        _ffn_kernel,
        grid=(n // tm,),
        in_specs=[
            pl.BlockSpec((tm, d), lambda i: (i, 0)),
            _const_spec((1, d)),
            _const_spec((d, 2 * D_FF)),
            _const_spec((D_FF, d)),
        ],
        out_specs=pl.BlockSpec((tm, d), lambda i: (i, 0)),
        out_shape=jax.ShapeDtypeStruct((n, d), F32),
        scratch_shapes=[pltpu.VMEM((tm, D_FF), BF16)],
        compiler_params=pltpu.CompilerParams(
            dimension_semantics=("arbitrary",), vmem_limit_bytes=VMEM_LIMIT),
        name="ffn",
    )(x1, g, wi, wo)


def kernel(x, mem, norm1_g, w_in, conv_w, conv_b, fox_f_bias, fox_q_g, fox_k_g, mem_norm_g,
           w_mem_kv, xa_q_g, xa_k_g, w_br_conv, w_br_fox, w_br_xa, w_o, norm2_g, w_ffn_in,
           w_ffn_out):
    batch, seq, d = x.shape
    depth = norm1_g.shape[0]
    n_main = N_GROUPS * d
    x2 = x.reshape(batch * seq, d)
    for l in range(depth):
        w_ff = jnp.pad(w_in[l, :, n_main:], ((0, 0), (0, LANES - FOX_HEADS))).astype(BF16)
        f_bias = jnp.pad(fox_f_bias[l], (0, LANES - FOX_HEADS)).reshape(1, LANES)

        mk, mv = _mem_kv(mem, mem_norm_g[l].reshape(1, d), w_mem_kv[l].astype(BF16),
                         xa_k_g[l].reshape(1, XA_HEAD_DIM))
        z = _in_proj(x2, norm1_g[l].reshape(1, d), w_in[l], w_ff, f_bias,
                     fox_q_g[l].reshape(1, FOX_HEAD_DIM), fox_k_g[l].reshape(1, FOX_HEAD_DIM),
                     xa_q_g[l].reshape(1, XA_HEAD_DIM), conv_w[l], conv_b[l].reshape(1, d), seq)
        y_fox = _fox(z, batch, seq)
        x1 = _branch(z, y_fox, x2, mk, mv,
                     w_br_conv[l].astype(BF16), w_br_fox[l].astype(BF16),
                     w_br_xa[l].astype(BF16), w_o[l].astype(BF16), seq)
        x2 = _ffn(x1, norm2_g[l].reshape(1, d), w_ffn_in[l].astype(BF16),
                  w_ffn_out[l].astype(BF16))
    return x2.reshape(batch, seq, d)
```

```python
import functools
import math

import jax
import jax.numpy as jnp
from jax import lax
from jax.experimental import pallas as pl
from jax.experimental.pallas import tpu as pltpu

F32 = jnp.float32
BF16 = jnp.bfloat16

D_MODEL = 1024
N_GROUPS = 10
FOX_HEADS = 8
FOX_HEAD_DIM = 128
XA_HEADS = 4
XA_HEAD_DIM = 256
D_FF = 2816
FF_CHUNK = 256
CONV_WIDTH = 3
EPS = 1e-6
LOG2E = math.log2(math.e)
LANES = 128
CONV_HALO = 8

W_CB, W_CC, W_CV, W_FQ, W_FK, W_FV, W_XQ, W_GA, W_GB, W_GC = range(N_GROUPS)
G_YC, G_FQ, G_FK, G_FV, G_XQ, G_GA, G_GB, G_GC = range(N_GROUPS - 2)
DEC_COL0 = (N_GROUPS - 2) * D_MODEL
Z_WIDTH = DEC_COL0 + 2 * 128
DEC_K0 = 3 * FOX_HEADS

TM_IN = 512
IN_CHUNK = 256
W_STAGE_COLS = 512
TQ = 512
TK = 512
FOX_HEADS_PER_STEP = 2
FOX_ROW_CHUNK = 64
TM_BR = 512
TM_FFN = 1024
VMEM_LIMIT = 56 * 1024 * 1024
VMEM_LIMIT_IN_PROJ = 60 * 1024 * 1024


def _rms(x, g):
    ms = jnp.mean(x * x, axis=-1, keepdims=True)
    return x * lax.rsqrt(ms + EPS) * g


def _dot(a, b):
    return jnp.dot(a, b, preferred_element_type=F32)


def _dot_nt(a, b):
    return lax.dot_general(a, b, (((1,), (1,)), ((), ())), preferred_element_type=F32)


def _const_spec(shape):
    nd = len(shape)
    return pl.BlockSpec(shape, lambda *_: (0,) * nd, pipeline_mode=pl.Buffered(1))


def _mem_kv_kernel(mem_ref, g_ref, w_ref, kg_ref, k_ref, v_ref):
    mn = _rms(mem_ref[0], g_ref[...]).astype(BF16)
    kv = _dot(mn, w_ref[...])
    for h in range(XA_HEADS):
        hs = slice(h * XA_HEAD_DIM, (h + 1) * XA_HEAD_DIM)
        k_ref[0, :, hs] = _rms(kv[:, hs], kg_ref[...]).astype(BF16)
    v_ref[0] = kv[:, D_MODEL:].astype(BF16)


def _mem_kv(mem, g, w, kg):
    b, m, d = mem.shape
    return pl.pallas_call(
        _mem_kv_kernel,
        grid=(b,),
        in_specs=[
            pl.BlockSpec((1, m, d), lambda i: (i, 0, 0)),
            _const_spec((1, d)),
            _const_spec((d, 2 * d)),
            _const_spec((1, XA_HEAD_DIM)),
        ],
        out_specs=[
            pl.BlockSpec((1, m, d), lambda i: (i, 0, 0)),
            pl.BlockSpec((1, m, d), lambda i: (i, 0, 0)),
        ],
        out_shape=[jax.ShapeDtypeStruct((b, m, d), BF16)] * 2,
        compiler_params=pltpu.CompilerParams(
            dimension_semantics=("arbitrary",), vmem_limit_bytes=VMEM_LIMIT),
        name="mem_kv",
    )(mem, g, w, kg)


def _log_sigmoid(x):
    return jnp.minimum(x, 0.0) - jnp.log1p(jnp.exp(-jnp.abs(x)))


def _cumsum_rows(x):
    n = x.shape[0]
    row = lax.broadcasted_iota(jnp.int32, x.shape, 0)
    s = 1
    while s < n:
        x = x + jnp.where(row >= s, pltpu.roll(x, s, 0), 0.0)
        s *= 2
    return x


def _load_cast_weight(w_hbm, w_ref, stage_ref, sem_ref):
    n_chunks = w_ref.shape[1] // W_STAGE_COLS

    def chunk_copy(c):
        return pltpu.make_async_copy(
            w_hbm.at[:, pl.ds(c * W_STAGE_COLS, W_STAGE_COLS)],
            stage_ref.at[c % 2], sem_ref.at[c % 2])

    chunk_copy(0).start()
    for c in range(n_chunks):
        if c + 1 < n_chunks:
            chunk_copy(c + 1).start()
        chunk_copy(c).wait()
        w_ref[:, c * W_STAGE_COLS:(c + 1) * W_STAGE_COLS] = stage_ref[c % 2].astype(BF16)


def _in_proj_kernel(tiles_per_seq, layer, x_ref, g1_ref, w_hbm, wff_ref, fb_ref, qg_ref, kg_ref,
                    xqg_ref, psel_ref, cw_ref, cbias_ref, z_ref,
                    carry_ref, h_ref, cb_ref, cc_ref, ucarry_ref, w_ref, stage_ref, sem_ref):
    i = pl.program_id(0)
    tm = x_ref.shape[0]
    d = x_ref.shape[1]

    @pl.when(i == 0)
    def _():
        _load_cast_weight(w_hbm.at[layer], w_ref, stage_ref, sem_ref)

    h_ref[...] = _rms(x_ref[...], g1_ref[...]).astype(BF16)

    lf = _log_sigmoid(_dot(h_ref[...], wff_ref[...]) + fb_ref[...])

    @pl.when(i % tiles_per_seq == 0)
    def _():
        carry_ref[...] = jnp.zeros_like(carry_ref)
        ucarry_ref[...] = jnp.zeros_like(ucarry_ref)

    c = _cumsum_rows(lf) + carry_ref[...]
    carry_ref[...] = c[tm - 1:tm, :]
    c = c * LOG2E

    hi = c.astype(BF16)
    mid = (c - hi.astype(F32)).astype(BF16)
    lo = (c - hi.astype(F32) - mid.astype(F32)).astype(BF16)
    dec = _dot(jnp.concatenate([hi, mid, lo], axis=1), psel_ref[...])
    lane = lax.broadcasted_iota(jnp.int32, (tm, LANES), 1)
    q_ones = jnp.where((lane >= DEC_K0) & (lane < 2 * DEC_K0), 1.0, 0.0)
    z_ref[:, DEC_COL0:DEC_COL0 + LANES] = (dec[:, :LANES] + q_ones).astype(BF16)
    z_ref[:, DEC_COL0 + LANES:DEC_COL0 + 2 * LANES] = dec[:, LANES:].astype(BF16)

    def head_norm(r, g_ref, scale, col0):
        hd = g_ref.shape[1]
        for h in range(r.shape[1] // hd):
            z_ref[:, col0 + h * hd:col0 + (h + 1) * hd] = (
                _rms(r[:, h * hd:(h + 1) * hd], g_ref[...]) * scale).astype(BF16)

    hrow = lax.broadcasted_iota(jnp.int32, (CONV_HALO, IN_CHUNK), 0)

    def gated_conv(cv, cs):
        u = cc_ref[:, cs] * cv
        prev = ucarry_ref[:, cs]
        ucarry_ref[:, cs] = u[tm - CONV_HALO:, :]
        u1 = pltpu.roll(u, 1, 0)
        u2 = pltpu.roll(u, 2, 0)
        u1 = jnp.concatenate(
            [jnp.where(hrow < 1, pltpu.roll(prev, 1, 0), u1[:CONV_HALO]), u1[CONV_HALO:]], axis=0)
        u2 = jnp.concatenate(
            [jnp.where(hrow < 2, pltpu.roll(prev, 2, 0), u2[:CONV_HALO]), u2[CONV_HALO:]], axis=0)
        conv = cw_ref[0:1, cs] * u2 + cw_ref[1:2, cs] * u1 + cw_ref[2:3, cs] * u
        return cb_ref[:, cs] * (conv + cbias_ref[:, cs])

    n_chunks = N_GROUPS * d // IN_CHUNK
    for ci in range(n_chunks):
        col0 = ci * IN_CHUNK
        wg = col0 // d
        cs = slice(col0 % d, col0 % d + IN_CHUNK)
        zcol0 = col0 - (W_CV - G_YC) * d
        r = _dot(h_ref[...], w_ref[:, col0:col0 + IN_CHUNK])
        if wg == W_CB:
            cb_ref[:, cs] = r
        elif wg == W_CC:
            cc_ref[:, cs] = r
        elif wg == W_CV:
            z_ref[:, zcol0:zcol0 + IN_CHUNK] = gated_conv(r, cs).astype(BF16)
        elif wg == W_FQ:
            head_norm(r, qg_ref, LOG2E / math.sqrt(FOX_HEAD_DIM), zcol0)
        elif wg == W_FK:
            head_norm(r, kg_ref, 1.0, zcol0)
        elif wg == W_XQ:
            head_norm(r, xqg_ref, LOG2E / math.sqrt(XA_HEAD_DIM), zcol0)
        elif wg in (W_GA, W_GB, W_GC):
            z_ref[:, zcol0:zcol0 + IN_CHUNK] = jax.nn.sigmoid(r).astype(BF16)
        else:
            z_ref[:, zcol0:zcol0 + IN_CHUNK] = r.astype(BF16)


def _decay_selector():
    sel = [[0.0] * (2 * LANES) for _ in range(3 * LANES)]
    for h in range(FOX_HEADS):
        for j in range(3):
            sel[j * LANES + h][3 * h + j] = 1.0
            sel[j * LANES + h][LANES + DEC_K0 + 3 * h + j] = -1.0
    return jnp.array(sel, BF16)


def _in_proj(x2, g1, w_all, layer, wff, fb, qg, kg, xqg, cw, cbias, seq):
    n, d = x2.shape
    tm = TM_IN
    kern = functools.partial(_in_proj_kernel, seq // tm, layer)
    psel = _decay_selector()
    return pl.pallas_call(
        kern,
        grid=(n // tm,),
        in_specs=[
            pl.BlockSpec((tm, d), lambda i: (i, 0)),
            _const_spec((1, d)),
            pl.BlockSpec(memory_space=pl.ANY),
            _const_spec((d, LANES)),
            _const_spec((1, LANES)),
            _const_spec((1, FOX_HEAD_DIM)),
            _const_spec((1, FOX_HEAD_DIM)),
            _const_spec((1, XA_HEAD_DIM)),
            _const_spec(psel.shape),
            _const_spec((CONV_WIDTH, d)),
            _const_spec((1, d)),
        ],
        out_specs=pl.BlockSpec((tm, Z_WIDTH), lambda i: (i, 0)),
        out_shape=jax.ShapeDtypeStruct((n, Z_WIDTH), BF16),
        scratch_shapes=[
            pltpu.VMEM((1, LANES), F32),
            pltpu.VMEM((tm, d), BF16),
            pltpu.VMEM((tm, d), F32),
            pltpu.VMEM((tm, d), F32),
            pltpu.VMEM((CONV_HALO, d), F32),
            pltpu.VMEM((d, N_GROUPS * d), BF16),
            pltpu.VMEM((2, d, W_STAGE_COLS), F32),
            pltpu.SemaphoreType.DMA((2,)),
        ],
        compiler_params=pltpu.CompilerParams(
            dimension_semantics=("arbitrary",), vmem_limit_bytes=VMEM_LIMIT_IN_PROJ),
        name="in_proj",
    )(x2, g1, w_all, wff, fb, qg, kg, xqg, psel, cw, cbias)


def _fox_kernel(q_ref, qx_ref, k_ref, kx_ref, v_ref, o_ref,
                t_ref, p_ref, m_ref, alpha_ref, acc_ref, kxh_ref):
    qi = pl.program_id(1)
    row = lax.broadcasted_iota(jnp.int32, (TQ, TK), 0)
    col = lax.broadcasted_iota(jnp.int32, (TQ, TK), 1)
    causal = col <= row
    ones = jnp.ones((TK, FOX_HEAD_DIM), BF16)

    @pl.when(qi == 0)
    def _():
        lane = lax.broadcasted_iota(jnp.int32, kx_ref.shape, 1)
        kx = kx_ref[...]
        for h in range(FOX_HEADS):
            own_c = (lane >= DEC_K0 + 3 * h) & (lane < DEC_K0 + 3 * h + 3)
            pick_q = (lane >= 3 * h) & (lane < 3 * h + 3)
            kxh_ref[h] = jnp.where(own_c, kx, jnp.where(pick_q, 1.0, 0.0).astype(BF16))

    def row_parts(diagonal):
        if diagonal:
            return ((0, TQ // 2, TK // 2), (TQ // 2, TQ, TK))
        return ((0, TQ, TK),)

    def logits(h, s, kb, diagonal):
        hs = slice(h * FOX_HEAD_DIM, (h + 1) * FOX_HEAD_DIM)
        for r0, r1, width in row_parts(diagonal):
            keys = slice(kb * TK, kb * TK + width)
            t = _dot_nt(jnp.concatenate([q_ref[r0:r1, hs], qx_ref[r0:r1, :]], axis=1),
                        jnp.concatenate([k_ref[keys, hs], kxh_ref[h, keys, :]], axis=1))
            if diagonal:
                t = jnp.where(causal[r0:r1, :width], t, -jnp.inf)
            t_ref[kb % 2, s, r0:r1, :width] = t

    def accumulate(h, s, kb, diagonal):
        hs = slice(h * FOX_HEAD_DIM, (h + 1) * FOX_HEAD_DIM)
        par = kb % 2
        for r0, r1, width in row_parts(diagonal):
            for r in range(r0, r1, FOX_ROW_CHUNK):
                rows = slice(r, r + FOX_ROW_CHUNK)
                t = t_ref[par, s, rows, :width]
                m_new = jnp.broadcast_to(
                    jnp.max(t, axis=-1, keepdims=True), (FOX_ROW_CHUNK, LANES))
                if kb > 0:
                    m_old = m_ref[s, rows, :]
                    m_new = jnp.maximum(m_old, m_new)
                    alpha_ref[s, rows, :] = jnp.exp2(m_old - m_new)
                m_ref[s, rows, :] = m_new
                p_ref[par, s, rows, :width] = jnp.exp2(
                    t - jnp.tile(m_new, (1, width // LANES))).astype(BF16)
            keys = slice(kb * TK, kb * TK + width)
            v_ext = jnp.concatenate([v_ref[keys, hs], ones[:width]], axis=1)
            pv = _dot(p_ref[par, s, r0:r1, :width], v_ext)
            if kb > 0:
                pv += jnp.tile(alpha_ref[s, r0:r1, :], (1, 2)) * acc_ref[s, r0:r1, :]
            acc_ref[s, r0:r1, :] = pv

    def q_tile(n):
        for h0 in range(0, FOX_HEADS, FOX_HEADS_PER_STEP):
            heads = list(enumerate(range(h0, h0 + FOX_HEADS_PER_STEP)))
            for s, h in heads:
                logits(h, s, 0, n == 0)
            for kb in range(n + 1):
                if kb < n:
                    for s, h in heads:
                        logits(h, s, kb + 1, kb + 1 == n)
                for s, h in heads:
                    accumulate(h, s, kb, kb == n)
            for s, h in heads:
                hs = slice(h * FOX_HEAD_DIM, (h + 1) * FOX_HEAD_DIM)
                o_ref[:, hs] = (acc_ref[s, :, :FOX_HEAD_DIM]
                                / acc_ref[s, :, FOX_HEAD_DIM:]).astype(BF16)

    for n in range(k_ref.shape[0] // TQ):
        pl.when(qi == n)(functools.partial(q_tile, n))


def _fox(z, batch, seq):
    n = z.shape[0]
    d = D_MODEL
    nq = seq // TQ
    return pl.pallas_call(
        _fox_kernel,
        grid=(batch, nq),
        in_specs=[
            pl.BlockSpec((TQ, d), lambda b, q: (b * nq + q, G_FQ)),
            pl.BlockSpec((TQ, LANES), lambda b, q: (b * nq + q, DEC_COL0 // LANES)),
            pl.BlockSpec((seq, d), lambda b, q: (b, G_FK)),
            pl.BlockSpec((seq, LANES), lambda b, q: (b, DEC_COL0 // LANES + 1)),
            pl.BlockSpec((seq, d), lambda b, q: (b, G_FV)),
        ],
        out_specs=pl.BlockSpec((TQ, d), lambda b, q: (b * nq + q, 0)),
        out_shape=jax.ShapeDtypeStruct((n, d), BF16),
        scratch_shapes=[
            pltpu.VMEM((2, FOX_HEADS_PER_STEP, TQ, TK), F32),
            pltpu.VMEM((2, FOX_HEADS_PER_STEP, TQ, TK), BF16),
            pltpu.VMEM((FOX_HEADS_PER_STEP, TQ, LANES), F32),
            pltpu.VMEM((FOX_HEADS_PER_STEP, TQ, LANES), F32),
            pltpu.VMEM((FOX_HEADS_PER_STEP, TQ, 2 * FOX_HEAD_DIM), F32),
            pltpu.VMEM((FOX_HEADS, seq, LANES), BF16),
        ],
        compiler_params=pltpu.CompilerParams(
            dimension_semantics=("arbitrary", "arbitrary"), vmem_limit_bytes=VMEM_LIMIT),
        name="fox",
    )(z, z, z, z, z)


def _branch_kernel(yc_ref, xq_ref, ga_ref, gb_ref, gc_ref, yf_ref, x_ref, mk_ref, mv_ref,
                   wa_ref, wb_ref, wc_ref, wo_ref, o_ref):
    merged = ga_ref[...].astype(F32) * _dot(yc_ref[...], wa_ref[...])

    merged += gb_ref[...].astype(F32) * _dot(yf_ref[...], wb_ref[...])

    heads = []
    for h in range(XA_HEADS):
        hs = slice(h * XA_HEAD_DIM, (h + 1) * XA_HEAD_DIM)
        s = _dot_nt(xq_ref[:, hs], mk_ref[0, :, hs])
        p = jnp.exp2(s - jnp.max(s, axis=-1, keepdims=True))
        o = _dot(p.astype(BF16), mv_ref[0, :, hs])
        heads.append(o / jnp.sum(p, axis=-1, keepdims=True))
    y_xa = jnp.concatenate(heads, axis=-1).astype(BF16)
    merged += gc_ref[...].astype(F32) * _dot(y_xa, wc_ref[...])

    o_ref[...] = x_ref[...] + _dot(merged.astype(BF16), wo_ref[...])


def _branch(z, y_fox, x2, mk, mv, wa, wb, wc, wo, seq):
    n, d = x2.shape
    tm = TM_BR
    nt = n // tm
    tps = seq // tm
    n_mem = mk.shape[1]

    def zspec(g):
        return pl.BlockSpec((tm, d), lambda i: (i, g))

    return pl.pallas_call(
        _branch_kernel,
        grid=(nt,),
        in_specs=[
            zspec(G_YC), zspec(G_XQ), zspec(G_GA), zspec(G_GB), zspec(G_GC),
            pl.BlockSpec((tm, d), lambda i: (i, 0)),
            pl.BlockSpec((tm, d), lambda i: (i, 0)),
            pl.BlockSpec((1, n_mem, d), lambda i: (i // tps, 0, 0)),
            pl.BlockSpec((1, n_mem, d), lambda i: (i // tps, 0, 0)),
            _const_spec((d, d)), _const_spec((d, d)), _const_spec((d, d)), _const_spec((d, d)),
        ],
        out_specs=pl.BlockSpec((tm, d), lambda i: (i, 0)),
        out_shape=jax.ShapeDtypeStruct((n, d), F32),
        compiler_params=pltpu.CompilerParams(
            dimension_semantics=("arbitrary",), vmem_limit_bytes=VMEM_LIMIT),
        name="branch",
    )(z, z, z, z, z, y_fox, x2, mk, mv, wa, wb, wc, wo)


def _ffn_kernel(x_ref, g_ref, wi_ref, wo_ref, o_ref, a_ref):
    x = x_ref[...]
    h = _rms(x, g_ref[...]).astype(BF16)
    for c in range(D_FF // FF_CHUNK):
        g = _dot(h, wi_ref[:, c * FF_CHUNK:(c + 1) * FF_CHUNK])
        u = _dot(h, wi_ref[:, D_FF + c * FF_CHUNK:D_FF + (c + 1) * FF_CHUNK])
        a_ref[:, c * FF_CHUNK:(c + 1) * FF_CHUNK] = (g * jax.nn.sigmoid(g) * u).astype(BF16)
    o_ref[...] = x + _dot(a_ref[...], wo_ref[...])


def _ffn(x1, g, wi, wo):
    n, d = x1.shape
    tm = TM_FFN
    return pl.pallas_call(
        _ffn_kernel,
        grid=(n // tm,),
        in_specs=[
            pl.BlockSpec((tm, d), lambda i: (i, 0)),
            _const_spec((1, d)),
            _const_spec((d, 2 * D_FF)),
            _const_spec((D_FF, d)),
        ],
        out_specs=pl.BlockSpec((tm, d), lambda i: (i, 0)),
        out_shape=jax.ShapeDtypeStruct((n, d), F32),
        scratch_shapes=[pltpu.VMEM((tm, D_FF), BF16)],
        compiler_params=pltpu.CompilerParams(
            dimension_semantics=("arbitrary",), vmem_limit_bytes=VMEM_LIMIT),
        name="ffn",
    )(x1, g, wi, wo)


def kernel(x, mem, norm1_g, w_in, conv_w, conv_b, fox_f_bias, fox_q_g, fox_k_g, mem_norm_g,
           w_mem_kv, xa_q_g, xa_k_g, w_br_conv, w_br_fox, w_br_xa, w_o, norm2_g, w_ffn_in,
           w_ffn_out):
    batch, seq, d = x.shape
    depth = norm1_g.shape[0]
    n_main = N_GROUPS * d
    x2 = x.reshape(batch * seq, d)
    for l in range(depth):
        w_ff = jnp.pad(w_in[l, :, n_main:], ((0, 0), (0, LANES - FOX_HEADS))).astype(BF16)
        f_bias = jnp.pad(fox_f_bias[l], (0, LANES - FOX_HEADS)).reshape(1, LANES)

        mk, mv = _mem_kv(mem, mem_norm_g[l].reshape(1, d), w_mem_kv[l].astype(BF16),
                         xa_k_g[l].reshape(1, XA_HEAD_DIM))
        z = _in_proj(x2, norm1_g[l].reshape(1, d), w_in, l, w_ff, f_bias,
                     fox_q_g[l].reshape(1, FOX_HEAD_DIM), fox_k_g[l].reshape(1, FOX_HEAD_DIM),
                     xa_q_g[l].reshape(1, XA_HEAD_DIM), conv_w[l], conv_b[l].reshape(1, d), seq)
        y_fox = _fox(z, batch, seq)
        x1 = _branch(z, y_fox, x2, mk, mv,
                     w_br_conv[l].astype(BF16), w_br_fox[l].astype(BF16),
                     w_br_xa[l].astype(BF16), w_o[l].astype(BF16), seq)
        x2 = _ffn(x1, norm2_g[l].reshape(1, d), w_ffn_in[l].astype(BF16),
                  w_ffn_out[l].astype(BF16))
    return x2.reshape(batch, seq, d)
```

```python
import functools
import math

import jax
import jax.numpy as jnp
from jax import lax
from jax.experimental import pallas as pl
from jax.experimental.pallas import tpu as pltpu

F32 = jnp.float32
BF16 = jnp.bfloat16

D_MODEL = 1024
N_GROUPS = 10
FOX_HEADS = 8
FOX_HEAD_DIM = 128
XA_HEADS = 4
XA_HEAD_DIM = 256
D_FF = 2816
FF_CHUNK = 256
CONV_WIDTH = 3
EPS = 1e-6
LOG2E = math.log2(math.e)
LANES = 128
CONV_HALO = 8

W_CB, W_CC, W_CV, W_FQ, W_FK, W_FV, W_XQ, W_GA, W_GB, W_GC = range(N_GROUPS)
G_YC, G_FQ, G_FK, G_FV, G_XQ, G_GA, G_GB, G_GC = range(N_GROUPS - 2)
DEC_COL0 = (N_GROUPS - 2) * D_MODEL
Z_WIDTH = DEC_COL0 + 2 * 128
DEC_PARTS = 3
DEC_K0 = DEC_PARTS * FOX_HEADS

TM_IN = 512
IN_CHUNK = 256
TQ = 512
TK = 512
FOX_HEADS_PER_STEP = 2
FOX_ROW_CHUNK = 64
TM_BR = 512
TM_FFN = 1024
VMEM_LIMIT = 56 * 1024 * 1024
VMEM_LIMIT_IN_PROJ = 60 * 1024 * 1024


def _rms(x, g):
    ms = jnp.mean(x * x, axis=-1, keepdims=True)
    return x * lax.rsqrt(ms + EPS) * g


def _dot(a, b):
    return jnp.dot(a, b, preferred_element_type=F32)


def _dot_nt(a, b):
    return lax.dot_general(a, b, (((1,), (1,)), ((), ())), preferred_element_type=F32)


def _const_spec(shape):
    nd = len(shape)
    return pl.BlockSpec(shape, lambda *_: (0,) * nd, pipeline_mode=pl.Buffered(1))


def _mem_kv_kernel(mem_ref, g_ref, w_ref, kg_ref, k_ref, v_ref):
    mn = _rms(mem_ref[0], g_ref[...]).astype(BF16)
    kv = _dot(mn, w_ref[...])
    for h in range(XA_HEADS):
        hs = slice(h * XA_HEAD_DIM, (h + 1) * XA_HEAD_DIM)
        k_ref[0, :, hs] = _rms(kv[:, hs], kg_ref[...]).astype(BF16)
    v_ref[0] = kv[:, D_MODEL:].astype(BF16)


def _mem_kv(mem, g, w, kg):
    b, m, d = mem.shape
    return pl.pallas_call(
        _mem_kv_kernel,
        grid=(b,),
        in_specs=[
            pl.BlockSpec((1, m, d), lambda i: (i, 0, 0)),
            _const_spec((1, d)),
            _const_spec((d, 2 * d)),
            _const_spec((1, XA_HEAD_DIM)),
        ],
        out_specs=[
            pl.BlockSpec((1, m, d), lambda i: (i, 0, 0)),
            pl.BlockSpec((1, m, d), lambda i: (i, 0, 0)),
        ],
        out_shape=[jax.ShapeDtypeStruct((b, m, d), BF16)] * 2,
        compiler_params=pltpu.CompilerParams(
            dimension_semantics=("arbitrary",), vmem_limit_bytes=VMEM_LIMIT),
        name="mem_kv",
    )(mem, g, w, kg)


def _log_sigmoid(x):
    return jnp.minimum(x, 0.0) - jnp.log1p(jnp.exp(-jnp.abs(x)))


def _cumsum_rows(x):
    n = x.shape[0]
    row = lax.broadcasted_iota(jnp.int32, x.shape, 0)
    s = 1
    while s < n:
        x = x + jnp.where(row >= s, pltpu.roll(x, s, 0), 0.0)
        s *= 2
    return x


def _in_proj_kernel(tiles_per_seq, x_ref, g1_ref, w_ref, wff_ref, fb_ref, qg_ref, kg_ref,
                    xqg_ref, psel_ref, cw_ref, cbias_ref, z_ref,
                    carry_ref, h_ref, cb_ref, cc_ref, ucarry_ref):
    i = pl.program_id(0)
    tm = x_ref.shape[0]
    d = x_ref.shape[1]
    h_ref[...] = _rms(x_ref[...], g1_ref[...]).astype(BF16)

    lf = _log_sigmoid(_dot(h_ref[...], wff_ref[...]) + fb_ref[...])

    @pl.when(i % tiles_per_seq == 0)
    def _():
        carry_ref[...] = jnp.zeros_like(carry_ref)
        ucarry_ref[...] = jnp.zeros_like(ucarry_ref)

    c = _cumsum_rows(lf) + carry_ref[...]
    carry_ref[...] = c[tm - 1:tm, :]
    c = c * LOG2E

    hi = c.astype(BF16)
    mid = (c - hi.astype(F32)).astype(BF16)
    lo = (c - hi.astype(F32) - mid.astype(F32)).astype(BF16)
    dec = _dot(jnp.concatenate([hi, mid, lo], axis=1), psel_ref[...])
    lane = lax.broadcasted_iota(jnp.int32, (tm, LANES), 1)
    q_ones = jnp.where((lane >= DEC_K0) & (lane < 2 * DEC_K0), 1.0, 0.0)
    z_ref[:, DEC_COL0:DEC_COL0 + LANES] = (dec[:, :LANES] + q_ones).astype(BF16)
    z_ref[:, DEC_COL0 + LANES:DEC_COL0 + 2 * LANES] = dec[:, LANES:].astype(BF16)

    def head_norm(r, g_ref, scale, col0):
        hd = g_ref.shape[1]
        for h in range(r.shape[1] // hd):
            z_ref[:, col0 + h * hd:col0 + (h + 1) * hd] = (
                _rms(r[:, h * hd:(h + 1) * hd], g_ref[...]) * scale).astype(BF16)

    hrow = lax.broadcasted_iota(jnp.int32, (CONV_HALO, IN_CHUNK), 0)

    def gated_conv(cv, cs):
        u = cc_ref[:, cs] * cv
        prev = ucarry_ref[:, cs]
        ucarry_ref[:, cs] = u[tm - CONV_HALO:, :]
        u1 = pltpu.roll(u, 1, 0)
        u2 = pltpu.roll(u, 2, 0)
        u1 = jnp.concatenate(
            [jnp.where(hrow < 1, pltpu.roll(prev, 1, 0), u1[:CONV_HALO]), u1[CONV_HALO:]], axis=0)
        u2 = jnp.concatenate(
            [jnp.where(hrow < 2, pltpu.roll(prev, 2, 0), u2[:CONV_HALO]), u2[CONV_HALO:]], axis=0)
        conv = cw_ref[0:1, cs] * u2 + cw_ref[1:2, cs] * u1 + cw_ref[2:3, cs] * u
        return cb_ref[:, cs] * (conv + cbias_ref[:, cs])

    n_chunks = N_GROUPS * d // IN_CHUNK
    for ci in range(n_chunks):
        col0 = ci * IN_CHUNK
        wg = col0 // d
        cs = slice(col0 % d, col0 % d + IN_CHUNK)
        zcol0 = col0 - (W_CV - G_YC) * d
        r = _dot(h_ref[...], w_ref[:, col0:col0 + IN_CHUNK])
        if wg == W_CB:
            cb_ref[:, cs] = r
        elif wg == W_CC:
            cc_ref[:, cs] = r
        elif wg == W_CV:
            z_ref[:, zcol0:zcol0 + IN_CHUNK] = gated_conv(r, cs).astype(BF16)
        elif wg == W_FQ:
            head_norm(r, qg_ref, LOG2E / math.sqrt(FOX_HEAD_DIM), zcol0)
        elif wg == W_FK:
            head_norm(r, kg_ref, 1.0, zcol0)
        elif wg == W_XQ:
            head_norm(r, xqg_ref, LOG2E / math.sqrt(XA_HEAD_DIM), zcol0)
        elif wg in (W_GA, W_GB, W_GC):
            z_ref[:, zcol0:zcol0 + IN_CHUNK] = jax.nn.sigmoid(r).astype(BF16)
        else:
            z_ref[:, zcol0:zcol0 + IN_CHUNK] = r.astype(BF16)


def _decay_selector():
    sel = [[0.0] * (2 * LANES) for _ in range(DEC_PARTS * LANES)]
    for h in range(FOX_HEADS):
        for j in range(DEC_PARTS):
            sel[j * LANES + h][DEC_PARTS * h + j] = 1.0
            sel[j * LANES + h][LANES + DEC_K0 + DEC_PARTS * h + j] = -1.0
    return jnp.array(sel, BF16)


def _in_proj(x2, g1, w, wff, fb, qg, kg, xqg, cw, cbias, seq):
    n, d = x2.shape
    tm = TM_IN
    kern = functools.partial(_in_proj_kernel, seq // tm)
    psel = _decay_selector()
    return pl.pallas_call(
        kern,
        grid=(n // tm,),
        in_specs=[
            pl.BlockSpec((tm, d), lambda i: (i, 0)),
            _const_spec((1, d)),
            _const_spec(w.shape),
            _const_spec((d, LANES)),
            _const_spec((1, LANES)),
            _const_spec((1, FOX_HEAD_DIM)),
            _const_spec((1, FOX_HEAD_DIM)),
            _const_spec((1, XA_HEAD_DIM)),
            _const_spec(psel.shape),
            _const_spec((CONV_WIDTH, d)),
            _const_spec((1, d)),
        ],
        out_specs=pl.BlockSpec((tm, Z_WIDTH), lambda i: (i, 0)),
        out_shape=jax.ShapeDtypeStruct((n, Z_WIDTH), BF16),
        scratch_shapes=[
            pltpu.VMEM((1, LANES), F32),
            pltpu.VMEM((tm, d), BF16),
            pltpu.VMEM((tm, d), F32),
            pltpu.VMEM((tm, d), F32),
            pltpu.VMEM((CONV_HALO, d), F32),
        ],
        compiler_params=pltpu.CompilerParams(
            dimension_semantics=("arbitrary",), vmem_limit_bytes=VMEM_LIMIT_IN_PROJ),
        name="in_proj",
    )(x2, g1, w, wff, fb, qg, kg, xqg, psel, cw, cbias)


def _fox_kernel(q_ref, qx_ref, k_ref, kx_ref, v_ref, o_ref,
                t_ref, p_ref, m_ref, alpha_ref, acc_ref, kxh_ref):
    qi = pl.program_id(1)
    row = lax.broadcasted_iota(jnp.int32, (TQ, TK), 0)
    col = lax.broadcasted_iota(jnp.int32, (TQ, TK), 1)
    causal = col <= row
    ones = jnp.ones((TK, FOX_HEAD_DIM), BF16)

    @pl.when(qi == 0)
    def _():
        lane = lax.broadcasted_iota(jnp.int32, kx_ref.shape, 1)
        kx = kx_ref[...]
        for h in range(FOX_HEADS):
            own_c = (lane >= DEC_K0 + DEC_PARTS * h) & (lane < DEC_K0 + DEC_PARTS * (h + 1))
            pick_q = (lane >= DEC_PARTS * h) & (lane < DEC_PARTS * (h + 1))
            kxh_ref[h] = jnp.where(own_c, kx, jnp.where(pick_q, 1.0, 0.0).astype(BF16))

    def row_parts(diagonal):
        if diagonal:
            return ((0, TQ // 2, TK // 2), (TQ // 2, TQ, TK))
        return ((0, TQ, TK),)

    def logits(h, s, kb, diagonal):
        hs = slice(h * FOX_HEAD_DIM, (h + 1) * FOX_HEAD_DIM)
        for r0, r1, width in row_parts(diagonal):
            keys = slice(kb * TK, kb * TK + width)
            t = _dot_nt(jnp.concatenate([q_ref[r0:r1, hs], qx_ref[r0:r1, :]], axis=1),
                        jnp.concatenate([k_ref[keys, hs], kxh_ref[h, keys, :]], axis=1))
            if diagonal:
                t = jnp.where(causal[r0:r1, :width], t, -jnp.inf)
            t_ref[kb % 2, s, r0:r1, :width] = t

    def accumulate(h, s, kb, diagonal):
        hs = slice(h * FOX_HEAD_DIM, (h + 1) * FOX_HEAD_DIM)
        par = kb % 2
        for r0, r1, width in row_parts(diagonal):
            for r in range(r0, r1, FOX_ROW_CHUNK):
                rows = slice(r, r + FOX_ROW_CHUNK)
                t = t_ref[par, s, rows, :width]
                m_new = jnp.broadcast_to(
                    jnp.max(t, axis=-1, keepdims=True), (FOX_ROW_CHUNK, LANES))
                if kb > 0:
                    m_old = m_ref[s, rows, :]
                    m_new = jnp.maximum(m_old, m_new)
                    alpha_ref[s, rows, :] = jnp.exp2(m_old - m_new)
                m_ref[s, rows, :] = m_new
                p_ref[par, s, rows, :width] = jnp.exp2(
                    t - jnp.tile(m_new, (1, width // LANES))).astype(BF16)
            keys = slice(kb * TK, kb * TK + width)
            v_ext = jnp.concatenate([v_ref[keys, hs], ones[:width]], axis=1)
            pv = _dot(p_ref[par, s, r0:r1, :width], v_ext)
            if kb > 0:
                pv += jnp.tile(alpha_ref[s, r0:r1, :], (1, 2)) * acc_ref[s, r0:r1, :]
            acc_ref[s, r0:r1, :] = pv

    def q_tile(n):
        for h0 in range(0, FOX_HEADS, FOX_HEADS_PER_STEP):
            heads = list(enumerate(range(h0, h0 + FOX_HEADS_PER_STEP)))
            for s, h in heads:
                logits(h, s, 0, n == 0)
            for kb in range(n + 1):
                if kb < n:
                    for s, h in heads:
                        logits(h, s, kb + 1, kb + 1 == n)
                for s, h in heads:
                    accumulate(h, s, kb, kb == n)
            for s, h in heads:
                hs = slice(h * FOX_HEAD_DIM, (h + 1) * FOX_HEAD_DIM)
                o_ref[:, hs] = (acc_ref[s, :, :FOX_HEAD_DIM]
                                / acc_ref[s, :, FOX_HEAD_DIM:]).astype(BF16)

    for n in range(k_ref.shape[0] // TQ):
        pl.when(qi == n)(functools.partial(q_tile, n))


def _fox(z, batch, seq):
    n = z.shape[0]
    d = D_MODEL
    nq = seq // TQ
    return pl.pallas_call(
        _fox_kernel,
        grid=(batch, nq),
        in_specs=[
            pl.BlockSpec((TQ, d), lambda b, q: (b * nq + q, G_FQ)),
            pl.BlockSpec((TQ, LANES), lambda b, q: (b * nq + q, DEC_COL0 // LANES)),
            pl.BlockSpec((seq, d), lambda b, q: (b, G_FK)),
            pl.BlockSpec((seq, LANES), lambda b, q: (b, DEC_COL0 // LANES + 1)),
            pl.BlockSpec((seq, d), lambda b, q: (b, G_FV)),
        ],
        out_specs=pl.BlockSpec((TQ, d), lambda b, q: (b * nq + q, 0)),
        out_shape=jax.ShapeDtypeStruct((n, d), BF16),
        scratch_shapes=[
            pltpu.VMEM((2, FOX_HEADS_PER_STEP, TQ, TK), F32),
            pltpu.VMEM((2, FOX_HEADS_PER_STEP, TQ, TK), BF16),
            pltpu.VMEM((FOX_HEADS_PER_STEP, TQ, LANES), F32),
            pltpu.VMEM((FOX_HEADS_PER_STEP, TQ, LANES), F32),
            pltpu.VMEM((FOX_HEADS_PER_STEP, TQ, 2 * FOX_HEAD_DIM), F32),
            pltpu.VMEM((FOX_HEADS, seq, LANES), BF16),
        ],
        compiler_params=pltpu.CompilerParams(
            dimension_semantics=("arbitrary", "arbitrary"), vmem_limit_bytes=VMEM_LIMIT),
        name="fox",
    )(z, z, z, z, z)


def _branch_kernel(yc_ref, xq_ref, ga_ref, gb_ref, gc_ref, yf_ref, x_ref, mk_ref, mv_ref,
                   wa_ref, wb_ref, wc_ref, wo_ref, o_ref):
    merged = ga_ref[...].astype(F32) * _dot(yc_ref[...], wa_ref[...])

    merged += gb_ref[...].astype(F32) * _dot(yf_ref[...], wb_ref[...])

    heads = []
    for h in range(XA_HEADS):
        hs = slice(h * XA_HEAD_DIM, (h + 1) * XA_HEAD_DIM)
        s = _dot_nt(xq_ref[:, hs], mk_ref[0, :, hs])
        p = jnp.exp2(s - jnp.max(s, axis=-1, keepdims=True))
        o = _dot(p.astype(BF16), mv_ref[0, :, hs])
        heads.append(o / jnp.sum(p, axis=-1, keepdims=True))
    y_xa = jnp.concatenate(heads, axis=-1).astype(BF16)
    merged += gc_ref[...].astype(F32) * _dot(y_xa, wc_ref[...])

    o_ref[...] = x_ref[...] + _dot(merged.astype(BF16), wo_ref[...])


def _branch(z, y_fox, x2, mk, mv, wa, wb, wc, wo, seq):
    n, d = x2.shape
    tm = TM_BR
    nt = n // tm
    tps = seq // tm
    n_mem = mk.shape[1]

    def zspec(g):
        return pl.BlockSpec((tm, d), lambda i: (i, g))

    return pl.pallas_call(
        _branch_kernel,
        grid=(nt,),
        in_specs=[
            zspec(G_YC), zspec(G_XQ), zspec(G_GA), zspec(G_GB), zspec(G_GC),
            pl.BlockSpec((tm, d), lambda i: (i, 0)),
            pl.BlockSpec((tm, d), lambda i: (i, 0)),
            pl.BlockSpec((1, n_mem, d), lambda i: (i // tps, 0, 0)),
            pl.BlockSpec((1, n_mem, d), lambda i: (i // tps, 0, 0)),
            _const_spec((d, d)), _const_spec((d, d)), _const_spec((d, d)), _const_spec((d, d)),
        ],
        out_specs=pl.BlockSpec((tm, d), lambda i: (i, 0)),
        out_shape=jax.ShapeDtypeStruct((n, d), F32),
        compiler_params=pltpu.CompilerParams(
            dimension_semantics=("arbitrary",), vmem_limit_bytes=VMEM_LIMIT),
        name="branch",
    )(z, z, z, z, z, y_fox, x2, mk, mv, wa, wb, wc, wo)


def _ffn_kernel(x_ref, g_ref, wi_ref, wo_ref, o_ref, a_ref):
    x = x_ref[...]
    h = _rms(x, g_ref[...]).astype(BF16)
    for c in range(D_FF // FF_CHUNK):
        g = _dot(h, wi_ref[:, c * FF_CHUNK:(c + 1) * FF_CHUNK])
        u = _dot(h, wi_ref[:, D_FF + c * FF_CHUNK:D_FF + (c + 1) * FF_CHUNK])
        a_ref[:, c * FF_CHUNK:(c + 1) * FF_CHUNK] = (g * jax.nn.sigmoid(g) * u).astype(BF16)
    o_ref[...] = x + _dot(a_ref[...], wo_ref[...])


def _ffn(x1, g, wi, wo):
    n, d = x1.shape
    tm = TM_FFN
    return pl.pallas_call(
        _ffn_kernel,
        grid=(n // tm,),
        in_specs=[
            pl.BlockSpec((tm, d), lambda i: (i, 0)),
            _const_spec((1, d)),
            _const_spec((d, 2 * D_FF)),
            _const_spec((D_FF, d)),
        ],
        out_specs=pl.BlockSpec((tm, d), lambda i: (i, 0)),
        out_shape=jax.ShapeDtypeStruct((n, d), F32),
        scratch_shapes=[pltpu.VMEM((tm, D_FF), BF16)],
        compiler_params=pltpu.CompilerParams(
            dimension_semantics=("arbitrary",), vmem_limit_bytes=VMEM_LIMIT),
        name="ffn",
    )(x1, g, wi, wo)


def kernel(x, mem, norm1_g, w_in, conv_w, conv_b, fox_f_bias, fox_q_g, fox_k_g, mem_norm_g,
           w_mem_kv, xa_q_g, xa_k_g, w_br_conv, w_br_fox, w_br_xa, w_o, norm2_g, w_ffn_in,
           w_ffn_out):
    batch, seq, d = x.shape
    depth = norm1_g.shape[0]
    n_main = N_GROUPS * d
    x2 = x.reshape(batch * seq, d)
    for l in range(depth):
        w_main = w_in[l].astype(BF16)
        w_ff = jnp.pad(w_main[:, n_main:], ((0, 0), (0, LANES - FOX_HEADS)))
        f_bias = jnp.pad(fox_f_bias[l], (0, LANES - FOX_HEADS)).reshape(1, LANES)

        mk, mv = _mem_kv(mem, mem_norm_g[l].reshape(1, d), w_mem_kv[l].astype(BF16),
                         xa_k_g[l].reshape(1, XA_HEAD_DIM))
        z = _in_proj(x2, norm1_g[l].reshape(1, d), w_main, w_ff, f_bias,
                     fox_q_g[l].reshape(1, FOX_HEAD_DIM), fox_k_g[l].reshape(1, FOX_HEAD_DIM),
                     xa_q_g[l].reshape(1, XA_HEAD_DIM), conv_w[l], conv_b[l].reshape(1, d), seq)
        y_fox = _fox(z, batch, seq)
        x1 = _branch(z, y_fox, x2, mk, mv,
                     w_br_conv[l].astype(BF16), w_br_fox[l].astype(BF16),
                     w_br_xa[l].astype(BF16), w_o[l].astype(BF16), seq)
        x2 = _ffn(x1, norm2_g[l].reshape(1, d), w_ffn_in[l].astype(BF16),
                  w_ffn_out[l].astype(BF16))
    return x2.reshape(batch, seq, d)
```

```python
import functools
import math

import jax
import jax.numpy as jnp
from jax import lax
from jax.experimental import pallas as pl
from jax.experimental.pallas import tpu as pltpu

F32 = jnp.float32
BF16 = jnp.bfloat16

D_MODEL = 1024
N_GROUPS = 10
FOX_HEADS = 8
FOX_HEAD_DIM = 128
XA_HEADS = 4
XA_HEAD_DIM = 256
D_FF = 2816
FF_CHUNK = 256
CONV_WIDTH = 3
EPS = 1e-6
LOG2E = math.log2(math.e)
LANES = 128
CONV_HALO = 8

W_CB, W_CC, W_CV, W_FQ, W_FK, W_FV, W_XQ, W_GA, W_GB, W_GC = range(N_GROUPS)
G_YC, G_FQ, G_FK, G_FV, G_XQ, G_GA, G_GB, G_GC = range(N_GROUPS - 2)
DEC_COL0 = (N_GROUPS - 2) * D_MODEL
Z_WIDTH = DEC_COL0 + 2 * 128
DEC_PARTS = 3
DEC_K0 = DEC_PARTS * FOX_HEADS

TM_IN = 512
IN_CHUNK = 256
TQ = 512
TK = 512
FOX_HEADS_PER_STEP = 2
FOX_ROW_CHUNK = 64
TM_BR = 512
TM_FFN = 1024
VMEM_LIMIT = 56 * 1024 * 1024
VMEM_LIMIT_IN_PROJ = 60 * 1024 * 1024


def _rms(x, g):
    ms = jnp.mean(x * x, axis=-1, keepdims=True)
    return x * lax.rsqrt(ms + EPS) * g


def _sigmoid(x):
    return 0.5 * jnp.tanh(0.5 * x) + 0.5


def _dot(a, b):
    return jnp.dot(a, b, preferred_element_type=F32)


def _dot_nt(a, b):
    return lax.dot_general(a, b, (((1,), (1,)), ((), ())), preferred_element_type=F32)


def _const_spec(shape):
    nd = len(shape)
    return pl.BlockSpec(shape, lambda *_: (0,) * nd, pipeline_mode=pl.Buffered(1))


def _mem_kv_kernel(mem_ref, g_ref, w_ref, kg_ref, k_ref, v_ref):
    mn = _rms(mem_ref[0], g_ref[...]).astype(BF16)
    kv = _dot(mn, w_ref[...])
    for h in range(XA_HEADS):
        hs = slice(h * XA_HEAD_DIM, (h + 1) * XA_HEAD_DIM)
        k_ref[0, :, hs] = _rms(kv[:, hs], kg_ref[...]).astype(BF16)
    v_ref[0] = kv[:, D_MODEL:].astype(BF16)


def _mem_kv(mem, g, w, kg):
    b, m, d = mem.shape
    return pl.pallas_call(
        _mem_kv_kernel,
        grid=(b,),
        in_specs=[
            pl.BlockSpec((1, m, d), lambda i: (i, 0, 0)),
            _const_spec((1, d)),
            _const_spec((d, 2 * d)),
            _const_spec((1, XA_HEAD_DIM)),
        ],
        out_specs=[
            pl.BlockSpec((1, m, d), lambda i: (i, 0, 0)),
            pl.BlockSpec((1, m, d), lambda i: (i, 0, 0)),
        ],
        out_shape=[jax.ShapeDtypeStruct((b, m, d), BF16)] * 2,
        compiler_params=pltpu.CompilerParams(
            dimension_semantics=("arbitrary",), vmem_limit_bytes=VMEM_LIMIT),
        name="mem_kv",
    )(mem, g, w, kg)


def _log_sigmoid(x):
    return jnp.minimum(x, 0.0) - jnp.log1p(jnp.exp(-jnp.abs(x)))


def _cumsum_rows(x):
    n = x.shape[0]
    row = lax.broadcasted_iota(jnp.int32, x.shape, 0)
    s = 1
    while s < n:
        x = x + jnp.where(row >= s, pltpu.roll(x, s, 0), 0.0)
        s *= 2
    return x


def _in_proj_kernel(tiles_per_seq, x_ref, g1_ref, w_ref, wff_ref, fb_ref, qg_ref, kg_ref,
                    xqg_ref, psel_ref, cw_ref, cbias_ref, z_ref,
                    carry_ref, h_ref, cb_ref, cc_ref, ucarry_ref):
    i = pl.program_id(0)
    tm = x_ref.shape[0]
    d = x_ref.shape[1]
    h_ref[...] = _rms(x_ref[...], g1_ref[...]).astype(BF16)

    lf = _log_sigmoid(_dot(h_ref[...], wff_ref[...]) + fb_ref[...])

    @pl.when(i % tiles_per_seq == 0)
    def _():
        carry_ref[...] = jnp.zeros_like(carry_ref)
        ucarry_ref[...] = jnp.zeros_like(ucarry_ref)

    c = _cumsum_rows(lf) + carry_ref[...]
    carry_ref[...] = c[tm - 1:tm, :]
    c = c * LOG2E

    hi = c.astype(BF16)
    mid = (c - hi.astype(F32)).astype(BF16)
    lo = (c - hi.astype(F32) - mid.astype(F32)).astype(BF16)
    dec = _dot(jnp.concatenate([hi, mid, lo], axis=1), psel_ref[...])
    lane = lax.broadcasted_iota(jnp.int32, (tm, LANES), 1)
    q_ones = jnp.where((lane >= DEC_K0) & (lane < 2 * DEC_K0), 1.0, 0.0)
    z_ref[:, DEC_COL0:DEC_COL0 + LANES] = (dec[:, :LANES] + q_ones).astype(BF16)
    z_ref[:, DEC_COL0 + LANES:DEC_COL0 + 2 * LANES] = dec[:, LANES:].astype(BF16)

    def head_norm(r, g_ref, scale, col0):
        hd = g_ref.shape[1]
        for h in range(r.shape[1] // hd):
            z_ref[:, col0 + h * hd:col0 + (h + 1) * hd] = (
                _rms(r[:, h * hd:(h + 1) * hd], g_ref[...]) * scale).astype(BF16)

    hrow = lax.broadcasted_iota(jnp.int32, (CONV_HALO, IN_CHUNK), 0)

    def gated_conv(cv, cs):
        u = cc_ref[:, cs] * cv
        prev = ucarry_ref[:, cs]
        ucarry_ref[:, cs] = u[tm - CONV_HALO:, :]
        u1 = pltpu.roll(u, 1, 0)
        u2 = pltpu.roll(u, 2, 0)
        u1 = jnp.concatenate(
            [jnp.where(hrow < 1, pltpu.roll(prev, 1, 0), u1[:CONV_HALO]), u1[CONV_HALO:]], axis=0)
        u2 = jnp.concatenate(
            [jnp.where(hrow < 2, pltpu.roll(prev, 2, 0), u2[:CONV_HALO]), u2[CONV_HALO:]], axis=0)
        conv = cw_ref[0:1, cs] * u2 + cw_ref[1:2, cs] * u1 + cw_ref[2:3, cs] * u
        return cb_ref[:, cs] * (conv + cbias_ref[:, cs])

    n_chunks = N_GROUPS * d // IN_CHUNK
    for ci in range(n_chunks):
        col0 = ci * IN_CHUNK
        wg = col0 // d
        cs = slice(col0 % d, col0 % d + IN_CHUNK)
        zcol0 = col0 - (W_CV - G_YC) * d
        r = _dot(h_ref[...], w_ref[:, col0:col0 + IN_CHUNK])
        if wg == W_CB:
            cb_ref[:, cs] = r
        elif wg == W_CC:
            cc_ref[:, cs] = r
        elif wg == W_CV:
            z_ref[:, zcol0:zcol0 + IN_CHUNK] = gated_conv(r, cs).astype(BF16)
        elif wg == W_FQ:
            head_norm(r, qg_ref, LOG2E / math.sqrt(FOX_HEAD_DIM), zcol0)
        elif wg == W_FK:
            head_norm(r, kg_ref, 1.0, zcol0)
        elif wg == W_XQ:
            head_norm(r, xqg_ref, LOG2E / math.sqrt(XA_HEAD_DIM), zcol0)
        elif wg in (W_GA, W_GB, W_GC):
            z_ref[:, zcol0:zcol0 + IN_CHUNK] = _sigmoid(r).astype(BF16)
        else:
            z_ref[:, zcol0:zcol0 + IN_CHUNK] = r.astype(BF16)


def _decay_selector():
    sel = [[0.0] * (2 * LANES) for _ in range(DEC_PARTS * LANES)]
    for h in range(FOX_HEADS):
        for j in range(DEC_PARTS):
            sel[j * LANES + h][DEC_PARTS * h + j] = 1.0
            sel[j * LANES + h][LANES + DEC_K0 + DEC_PARTS * h + j] = -1.0
    return jnp.array(sel, BF16)


def _in_proj(x2, g1, w, wff, fb, qg, kg, xqg, cw, cbias, seq):
    n, d = x2.shape
    tm = TM_IN
    kern = functools.partial(_in_proj_kernel, seq // tm)
    psel = _decay_selector()
    return pl.pallas_call(
        kern,
        grid=(n // tm,),
        in_specs=[
            pl.BlockSpec((tm, d), lambda i: (i, 0)),
            _const_spec((1, d)),
            _const_spec(w.shape),
            _const_spec((d, LANES)),
            _const_spec((1, LANES)),
            _const_spec((1, FOX_HEAD_DIM)),
            _const_spec((1, FOX_HEAD_DIM)),
            _const_spec((1, XA_HEAD_DIM)),
            _const_spec(psel.shape),
            _const_spec((CONV_WIDTH, d)),
            _const_spec((1, d)),
        ],
        out_specs=pl.BlockSpec((tm, Z_WIDTH), lambda i: (i, 0)),
        out_shape=jax.ShapeDtypeStruct((n, Z_WIDTH), BF16),
        scratch_shapes=[
            pltpu.VMEM((1, LANES), F32),
            pltpu.VMEM((tm, d), BF16),
            pltpu.VMEM((tm, d), F32),
            pltpu.VMEM((tm, d), F32),
            pltpu.VMEM((CONV_HALO, d), F32),
        ],
        compiler_params=pltpu.CompilerParams(
            dimension_semantics=("arbitrary",), vmem_limit_bytes=VMEM_LIMIT_IN_PROJ),
        name="in_proj",
    )(x2, g1, w, wff, fb, qg, kg, xqg, psel, cw, cbias)


def _fox_kernel(q_ref, qx_ref, k_ref, kx_ref, v_ref, o_ref,
                t_ref, p_ref, m_ref, alpha_ref, acc_ref, kxh_ref):
    qi = pl.program_id(1)
    row = lax.broadcasted_iota(jnp.int32, (TQ, TK), 0)
    col = lax.broadcasted_iota(jnp.int32, (TQ, TK), 1)
    causal = col <= row
    ones = jnp.ones((TK, FOX_HEAD_DIM), BF16)

    @pl.when(qi == 0)
    def _():
        lane = lax.broadcasted_iota(jnp.int32, kx_ref.shape, 1)
        kx = kx_ref[...]
        for h in range(FOX_HEADS):
            own_c = (lane >= DEC_K0 + DEC_PARTS * h) & (lane < DEC_K0 + DEC_PARTS * (h + 1))
            pick_q = (lane >= DEC_PARTS * h) & (lane < DEC_PARTS * (h + 1))
            kxh_ref[h] = jnp.where(own_c, kx, jnp.where(pick_q, 1.0, 0.0).astype(BF16))

    def row_parts(diagonal):
        if diagonal:
            return ((0, TQ // 2, TK // 2), (TQ // 2, TQ, TK))
        return ((0, TQ, TK),)

    def logits(h, s, kb, diagonal):
        hs = slice(h * FOX_HEAD_DIM, (h + 1) * FOX_HEAD_DIM)
        for r0, r1, width in row_parts(diagonal):
            keys = slice(kb * TK, kb * TK + width)
            t = _dot_nt(jnp.concatenate([q_ref[r0:r1, hs], qx_ref[r0:r1, :]], axis=1),
                        jnp.concatenate([k_ref[keys, hs], kxh_ref[h, keys, :]], axis=1))
            if diagonal:
                t = jnp.where(causal[r0:r1, :width], t, -jnp.inf)
            t_ref[kb % 2, s, r0:r1, :width] = t

    def accumulate(h, s, kb, diagonal):
        hs = slice(h * FOX_HEAD_DIM, (h + 1) * FOX_HEAD_DIM)
        par = kb % 2
        for r0, r1, width in row_parts(diagonal):
            for r in range(r0, r1, FOX_ROW_CHUNK):
                rows = slice(r, r + FOX_ROW_CHUNK)
                t = t_ref[par, s, rows, :width]
                m_new = jnp.broadcast_to(
                    jnp.max(t, axis=-1, keepdims=True), (FOX_ROW_CHUNK, LANES))
                if kb > 0:
                    m_old = m_ref[s, rows, :]
                    m_new = jnp.maximum(m_old, m_new)
                    alpha_ref[s, rows, :] = jnp.exp2(m_old - m_new)
                m_ref[s, rows, :] = m_new
                p_ref[par, s, rows, :width] = jnp.exp2(
                    t - jnp.tile(m_new, (1, width // LANES))).astype(BF16)
            keys = slice(kb * TK, kb * TK + width)
            v_ext = jnp.concatenate([v_ref[keys, hs], ones[:width]], axis=1)
            pv = _dot(p_ref[par, s, r0:r1, :width], v_ext)
            if kb > 0:
                pv += jnp.tile(alpha_ref[s, r0:r1, :], (1, 2)) * acc_ref[s, r0:r1, :]
            acc_ref[s, r0:r1, :] = pv

    def q_tile(n):
        for h0 in range(0, FOX_HEADS, FOX_HEADS_PER_STEP):
            heads = list(enumerate(range(h0, h0 + FOX_HEADS_PER_STEP)))
            for s, h in heads:
                logits(h, s, 0, n == 0)
            for kb in range(n + 1):
                if kb < n:
                    for s, h in heads:
                        logits(h, s, kb + 1, kb + 1 == n)
                for s, h in heads:
                    accumulate(h, s, kb, kb == n)
            for s, h in heads:
                hs = slice(h * FOX_HEAD_DIM, (h + 1) * FOX_HEAD_DIM)
                o_ref[:, hs] = (acc_ref[s, :, :FOX_HEAD_DIM]
                                / acc_ref[s, :, FOX_HEAD_DIM:]).astype(BF16)

    for n in range(k_ref.shape[0] // TQ):
        pl.when(qi == n)(functools.partial(q_tile, n))


def _fox(z, batch, seq):
    n = z.shape[0]
    d = D_MODEL
    nq = seq // TQ
    return pl.pallas_call(
        _fox_kernel,
        grid=(batch, nq),
        in_specs=[
            pl.BlockSpec((TQ, d), lambda b, q: (b * nq + q, G_FQ)),
            pl.BlockSpec((TQ, LANES), lambda b, q: (b * nq + q, DEC_COL0 // LANES)),
            pl.BlockSpec((seq, d), lambda b, q: (b, G_FK)),
            pl.BlockSpec((seq, LANES), lambda b, q: (b, DEC_COL0 // LANES + 1)),
            pl.BlockSpec((seq, d), lambda b, q: (b, G_FV)),
        ],
        out_specs=pl.BlockSpec((TQ, d), lambda b, q: (b * nq + q, 0)),
        out_shape=jax.ShapeDtypeStruct((n, d), BF16),
        scratch_shapes=[
            pltpu.VMEM((2, FOX_HEADS_PER_STEP, TQ, TK), F32),
            pltpu.VMEM((2, FOX_HEADS_PER_STEP, TQ, TK), BF16),
            pltpu.VMEM((FOX_HEADS_PER_STEP, TQ, LANES), F32),
            pltpu.VMEM((FOX_HEADS_PER_STEP, TQ, LANES), F32),
            pltpu.VMEM((FOX_HEADS_PER_STEP, TQ, 2 * FOX_HEAD_DIM), F32),
            pltpu.VMEM((FOX_HEADS, seq, LANES), BF16),
        ],
        compiler_params=pltpu.CompilerParams(
            dimension_semantics=("arbitrary", "arbitrary"), vmem_limit_bytes=VMEM_LIMIT),
        name="fox",
    )(z, z, z, z, z)


def _branch_kernel(yc_ref, xq_ref, ga_ref, gb_ref, gc_ref, yf_ref, x_ref, mk_ref, mv_ref,
                   wa_ref, wb_ref, wc_ref, wo_ref, o_ref):
    merged = ga_ref[...].astype(F32) * _dot(yc_ref[...], wa_ref[...])

    merged += gb_ref[...].astype(F32) * _dot(yf_ref[...], wb_ref[...])

    heads = []
    for h in range(XA_HEADS):
        hs = slice(h * XA_HEAD_DIM, (h + 1) * XA_HEAD_DIM)
        s = _dot_nt(xq_ref[:, hs], mk_ref[0, :, hs])
        p = jnp.exp2(s - jnp.max(s, axis=-1, keepdims=True))
        o = _dot(p.astype(BF16), mv_ref[0, :, hs])
        heads.append(o / jnp.sum(p, axis=-1, keepdims=True))
    y_xa = jnp.concatenate(heads, axis=-1).astype(BF16)
    merged += gc_ref[...].astype(F32) * _dot(y_xa, wc_ref[...])

    o_ref[...] = x_ref[...] + _dot(merged.astype(BF16), wo_ref[...])


def _branch(z, y_fox, x2, mk, mv, wa, wb, wc, wo, seq):
    n, d = x2.shape
    tm = TM_BR
    nt = n // tm
    tps = seq // tm
    n_mem = mk.shape[1]

    def zspec(g):
        return pl.BlockSpec((tm, d), lambda i: (i, g))

    return pl.pallas_call(
        _branch_kernel,
        grid=(nt,),
        in_specs=[
            zspec(G_YC), zspec(G_XQ), zspec(G_GA), zspec(G_GB), zspec(G_GC),
            pl.BlockSpec((tm, d), lambda i: (i, 0)),
            pl.BlockSpec((tm, d), lambda i: (i, 0)),
            pl.BlockSpec((1, n_mem, d), lambda i: (i // tps, 0, 0)),
            pl.BlockSpec((1, n_mem, d), lambda i: (i // tps, 0, 0)),
            _const_spec((d, d)), _const_spec((d, d)), _const_spec((d, d)), _const_spec((d, d)),
        ],
        out_specs=pl.BlockSpec((tm, d), lambda i: (i, 0)),
        out_shape=jax.ShapeDtypeStruct((n, d), F32),
        compiler_params=pltpu.CompilerParams(
            dimension_semantics=("arbitrary",), vmem_limit_bytes=VMEM_LIMIT),
        name="branch",
    )(z, z, z, z, z, y_fox, x2, mk, mv, wa, wb, wc, wo)


def _ffn_kernel(x_ref, g_ref, wi_ref, wo_ref, o_ref, a_ref):
    x = x_ref[...]
    h = _rms(x, g_ref[...]).astype(BF16)
    for c in range(D_FF // FF_CHUNK):
        g = _dot(h, wi_ref[:, c * FF_CHUNK:(c + 1) * FF_CHUNK])
        u = _dot(h, wi_ref[:, D_FF + c * FF_CHUNK:D_FF + (c + 1) * FF_CHUNK])
        hg = 0.5 * g
        a_ref[:, c * FF_CHUNK:(c + 1) * FF_CHUNK] = ((hg * jnp.tanh(hg) + hg) * u).astype(BF16)
    o_ref[...] = x + _dot(a_ref[...], wo_ref[...])


def _ffn(x1, g, wi, wo):
    n, d = x1.shape
    tm = TM_FFN
    return pl.pallas_call(
        _ffn_kernel,
        grid=(n // tm,),
        in_specs=[
            pl.BlockSpec((tm, d), lambda i: (i, 0)),
            _const_spec((1, d)),
            _const_spec((d, 2 * D_FF)),
            _const_spec((D_FF, d)),
        ],
        out_specs=pl.BlockSpec((tm, d), lambda i: (i, 0)),
        out_shape=jax.ShapeDtypeStruct((n, d), F32),
        scratch_shapes=[pltpu.VMEM((tm, D_FF), BF16)],
        compiler_params=pltpu.CompilerParams(
            dimension_semantics=("arbitrary",), vmem_limit_bytes=VMEM_LIMIT),
        name="ffn",
    )(x1, g, wi, wo)


def kernel(x, mem, norm1_g, w_in, conv_w, conv_b, fox_f_bias, fox_q_g, fox_k_g, mem_norm_g,
           w_mem_kv, xa_q_g, xa_k_g, w_br_conv, w_br_fox, w_br_xa, w_o, norm2_g, w_ffn_in,
           w_ffn_out):
    batch, seq, d = x.shape
    depth = norm1_g.shape[0]
    n_main = N_GROUPS * d
    x2 = x.reshape(batch * seq, d)
    for l in range(depth):
        w_main = w_in[l].astype(BF16)
        w_ff = jnp.pad(w_main[:, n_main:], ((0, 0), (0, LANES - FOX_HEADS)))
        f_bias = jnp.pad(fox_f_bias[l], (0, LANES - FOX_HEADS)).reshape(1, LANES)

        mk, mv = _mem_kv(mem, mem_norm_g[l].reshape(1, d), w_mem_kv[l].astype(BF16),
                         xa_k_g[l].reshape(1, XA_HEAD_DIM))
        z = _in_proj(x2, norm1_g[l].reshape(1, d), w_main, w_ff, f_bias,
                     fox_q_g[l].reshape(1, FOX_HEAD_DIM), fox_k_g[l].reshape(1, FOX_HEAD_DIM),
                     xa_q_g[l].reshape(1, XA_HEAD_DIM), conv_w[l], conv_b[l].reshape(1, d), seq)
        y_fox = _fox(z, batch, seq)
        x1 = _branch(z, y_fox, x2, mk, mv,
                     w_br_conv[l].astype(BF16), w_br_fox[l].astype(BF16),
                     w_br_xa[l].astype(BF16), w_o[l].astype(BF16), seq)
        x2 = _ffn(x1, norm2_g[l].reshape(1, d), w_ffn_in[l].astype(BF16),
                  w_ffn_out[l].astype(BF16))
    return x2.reshape(batch, seq, d)
```

```python
import functools
import math

import jax
import jax.numpy as jnp
from jax import lax
from jax.experimental import pallas as pl
from jax.experimental.pallas import tpu as pltpu

F32 = jnp.float32
BF16 = jnp.bfloat16

D_MODEL = 1024
N_GROUPS = 10
FOX_HEADS = 8
FOX_HEAD_DIM = 128
XA_HEADS = 4
XA_HEAD_DIM = 256
D_FF = 2816
FF_CHUNK = 256
CONV_WIDTH = 3
EPS = 1e-6
LOG2E = math.log2(math.e)
LANES = 128
CONV_HALO = 8

W_CB, W_CC, W_CV, W_FQ, W_FK, W_FV, W_XQ, W_GA, W_GB, W_GC = range(N_GROUPS)
G_YC, G_FQ, G_FK, G_FV, G_XQ, G_GA, G_GB, G_GC = range(N_GROUPS - 2)
DEC_COL0 = (N_GROUPS - 2) * D_MODEL
Z_WIDTH = DEC_COL0 + 2 * 128
DEC_PARTS = 3
DEC_K0 = DEC_PARTS * FOX_HEADS

TM_IN = 512
IN_CHUNK = 256
TQ = 512
TK = 512
FOX_HEADS_PER_STEP = 2
FOX_ROW_CHUNK = 64
TM_BR = 512
TM_FFN = 1024
VMEM_LIMIT = 56 * 1024 * 1024
VMEM_LIMIT_IN_PROJ = 60 * 1024 * 1024


def _rms(x, g):
    ms = jnp.mean(x * x, axis=-1, keepdims=True)
    return x * lax.rsqrt(ms + EPS) * g


def _dot(a, b):
    return jnp.dot(a, b, preferred_element_type=F32)


def _dot_nt(a, b):
    return lax.dot_general(a, b, (((1,), (1,)), ((), ())), preferred_element_type=F32)


def _const_spec(shape):
    nd = len(shape)
    return pl.BlockSpec(shape, lambda *_: (0,) * nd, pipeline_mode=pl.Buffered(1))


def _mem_kv_kernel(mem_ref, g_ref, w_ref, kg_ref, k_ref, v_ref):
    mn = _rms(mem_ref[0], g_ref[...]).astype(BF16)
    kv = _dot(mn, w_ref[...])
    for h in range(XA_HEADS):
        hs = slice(h * XA_HEAD_DIM, (h + 1) * XA_HEAD_DIM)
        k_ref[0, :, hs] = _rms(kv[:, hs], kg_ref[...]).astype(BF16)
    v_ref[0] = kv[:, D_MODEL:].astype(BF16)


def _mem_kv(mem, g, w, kg):
    b, m, d = mem.shape
    return pl.pallas_call(
        _mem_kv_kernel,
        grid=(b,),
        in_specs=[
            pl.BlockSpec((1, m, d), lambda i: (i, 0, 0)),
            _const_spec((1, d)),
            _const_spec((d, 2 * d)),
            _const_spec((1, XA_HEAD_DIM)),
        ],
        out_specs=[
            pl.BlockSpec((1, m, d), lambda i: (i, 0, 0)),
            pl.BlockSpec((1, m, d), lambda i: (i, 0, 0)),
        ],
        out_shape=[jax.ShapeDtypeStruct((b, m, d), BF16)] * 2,
        compiler_params=pltpu.CompilerParams(
            dimension_semantics=("arbitrary",), vmem_limit_bytes=VMEM_LIMIT),
        name="mem_kv",
    )(mem, g, w, kg)


def _log_sigmoid(x):
    return jnp.minimum(x, 0.0) - jnp.log1p(jnp.exp(-jnp.abs(x)))


def _cumsum_rows(x):
    n = x.shape[0]
    row = lax.broadcasted_iota(jnp.int32, x.shape, 0)
    s = 1
    while s < n:
        x = x + jnp.where(row >= s, pltpu.roll(x, s, 0), 0.0)
        s *= 2
    return x


def _in_proj_kernel(tiles_per_seq, x_ref, g1_ref, w_ref, wff_ref, fb_ref, qg_ref, kg_ref,
                    xqg_ref, psel_ref, cw_ref, cbias_ref, z_ref,
                    carry_ref, h_ref, ucarry_ref):
    i = pl.program_id(0)
    tm = x_ref.shape[0]
    d = x_ref.shape[1]
    h_ref[...] = _rms(x_ref[...], g1_ref[...]).astype(BF16)

    lf = _log_sigmoid(_dot(h_ref[...], wff_ref[...]) + fb_ref[...])

    @pl.when(i % tiles_per_seq == 0)
    def _():
        carry_ref[...] = jnp.zeros_like(carry_ref)
        ucarry_ref[...] = jnp.zeros_like(ucarry_ref)

    c = _cumsum_rows(lf) + carry_ref[...]
    carry_ref[...] = c[tm - 1:tm, :]
    c = c * LOG2E

    hi = c.astype(BF16)
    mid = (c - hi.astype(F32)).astype(BF16)
    lo = (c - hi.astype(F32) - mid.astype(F32)).astype(BF16)
    dec = _dot(jnp.concatenate([hi, mid, lo], axis=1), psel_ref[...])
    lane = lax.broadcasted_iota(jnp.int32, (tm, LANES), 1)
    q_ones = jnp.where((lane >= DEC_K0) & (lane < 2 * DEC_K0), 1.0, 0.0)
    z_ref[:, DEC_COL0:DEC_COL0 + LANES] = (dec[:, :LANES] + q_ones).astype(BF16)
    z_ref[:, DEC_COL0 + LANES:DEC_COL0 + 2 * LANES] = dec[:, LANES:].astype(BF16)

    def head_norm(r, g_ref, scale, col0):
        hd = g_ref.shape[1]
        for h in range(r.shape[1] // hd):
            z_ref[:, col0 + h * hd:col0 + (h + 1) * hd] = (
                _rms(r[:, h * hd:(h + 1) * hd], g_ref[...]) * scale).astype(BF16)

    hrow = lax.broadcasted_iota(jnp.int32, (CONV_HALO, IN_CHUNK), 0)

    def gated_conv(cb, cc, cv, cs):
        u = cc * cv
        prev = ucarry_ref[:, cs]
        ucarry_ref[:, cs] = u[tm - CONV_HALO:, :]
        u1 = pltpu.roll(u, 1, 0)
        u2 = pltpu.roll(u, 2, 0)
        u1 = jnp.concatenate(
            [jnp.where(hrow < 1, pltpu.roll(prev, 1, 0), u1[:CONV_HALO]), u1[CONV_HALO:]], axis=0)
        u2 = jnp.concatenate(
            [jnp.where(hrow < 2, pltpu.roll(prev, 2, 0), u2[:CONV_HALO]), u2[CONV_HALO:]], axis=0)
        conv = cw_ref[0:1, cs] * u2 + cw_ref[1:2, cs] * u1 + cw_ref[2:3, cs] * u
        return cb * (conv + cbias_ref[:, cs])

    def proj(col0):
        return _dot(h_ref[...], w_ref[:, col0:col0 + IN_CHUNK])

    for c0 in range(0, d, IN_CHUNK):
        cs = slice(c0, c0 + IN_CHUNK)
        y = gated_conv(proj(W_CB * d + c0), proj(W_CC * d + c0), proj(W_CV * d + c0), cs)
        z_ref[:, G_YC * d + c0:G_YC * d + c0 + IN_CHUNK] = y.astype(BF16)

    n_chunks = N_GROUPS * d // IN_CHUNK
    for ci in range((W_CV + 1) * d // IN_CHUNK, n_chunks):
        col0 = ci * IN_CHUNK
        wg = col0 // d
        zcol0 = col0 - (W_CV - G_YC) * d
        r = proj(col0)
        if wg == W_FQ:
            head_norm(r, qg_ref, LOG2E / math.sqrt(FOX_HEAD_DIM), zcol0)
        elif wg == W_FK:
            head_norm(r, kg_ref, 1.0, zcol0)
        elif wg == W_XQ:
            head_norm(r, xqg_ref, LOG2E / math.sqrt(XA_HEAD_DIM), zcol0)
        elif wg in (W_GA, W_GB, W_GC):
            z_ref[:, zcol0:zcol0 + IN_CHUNK] = jax.nn.sigmoid(r).astype(BF16)
        else:
            z_ref[:, zcol0:zcol0 + IN_CHUNK] = r.astype(BF16)


def _decay_selector():
    sel = [[0.0] * (2 * LANES) for _ in range(DEC_PARTS * LANES)]
    for h in range(FOX_HEADS):
        for j in range(DEC_PARTS):
            sel[j * LANES + h][DEC_PARTS * h + j] = 1.0
            sel[j * LANES + h][LANES + DEC_K0 + DEC_PARTS * h + j] = -1.0
    return jnp.array(sel, BF16)


def _in_proj(x2, g1, w, wff, fb, qg, kg, xqg, cw, cbias, seq):
    n, d = x2.shape
    tm = TM_IN
    kern = functools.partial(_in_proj_kernel, seq // tm)
    psel = _decay_selector()
    return pl.pallas_call(
        kern,
        grid=(n // tm,),
        in_specs=[
            pl.BlockSpec((tm, d), lambda i: (i, 0)),
            _const_spec((1, d)),
            _const_spec(w.shape),
            _const_spec((d, LANES)),
            _const_spec((1, LANES)),
            _const_spec((1, FOX_HEAD_DIM)),
            _const_spec((1, FOX_HEAD_DIM)),
            _const_spec((1, XA_HEAD_DIM)),
            _const_spec(psel.shape),
            _const_spec((CONV_WIDTH, d)),
            _const_spec((1, d)),
        ],
        out_specs=pl.BlockSpec((tm, Z_WIDTH), lambda i: (i, 0)),
        out_shape=jax.ShapeDtypeStruct((n, Z_WIDTH), BF16),
        scratch_shapes=[
            pltpu.VMEM((1, LANES), F32),
            pltpu.VMEM((tm, d), BF16),
            pltpu.VMEM((CONV_HALO, d), F32),
        ],
        compiler_params=pltpu.CompilerParams(
            dimension_semantics=("arbitrary",), vmem_limit_bytes=VMEM_LIMIT_IN_PROJ),
        name="in_proj",
    )(x2, g1, w, wff, fb, qg, kg, xqg, psel, cw, cbias)


def _fox_kernel(q_ref, qx_ref, k_ref, kx_ref, v_ref, o_ref,
                t_ref, p_ref, m_ref, alpha_ref, acc_ref, kxh_ref):
    qi = pl.program_id(1)
    row = lax.broadcasted_iota(jnp.int32, (TQ, TK), 0)
    col = lax.broadcasted_iota(jnp.int32, (TQ, TK), 1)
    causal = col <= row
    ones = jnp.ones((TK, FOX_HEAD_DIM), BF16)

    @pl.when(qi == 0)
    def _():
        lane = lax.broadcasted_iota(jnp.int32, kx_ref.shape, 1)
        kx = kx_ref[...]
        for h in range(FOX_HEADS):
            own_c = (lane >= DEC_K0 + DEC_PARTS * h) & (lane < DEC_K0 + DEC_PARTS * (h + 1))
            pick_q = (lane >= DEC_PARTS * h) & (lane < DEC_PARTS * (h + 1))
            kxh_ref[h] = jnp.where(own_c, kx, jnp.where(pick_q, 1.0, 0.0).astype(BF16))

    def row_parts(diagonal):
        if diagonal:
            return ((0, TQ // 2, TK // 2), (TQ // 2, TQ, TK))
        return ((0, TQ, TK),)

    def logits(h, s, kb, diagonal):
        hs = slice(h * FOX_HEAD_DIM, (h + 1) * FOX_HEAD_DIM)
        for r0, r1, width in row_parts(diagonal):
            keys = slice(kb * TK, kb * TK + width)
            t = _dot_nt(jnp.concatenate([q_ref[r0:r1, hs], qx_ref[r0:r1, :]], axis=1),
                        jnp.concatenate([k_ref[keys, hs], kxh_ref[h, keys, :]], axis=1))
            if diagonal:
                t = jnp.where(causal[r0:r1, :width], t, -jnp.inf)
            t_ref[kb % 2, s, r0:r1, :width] = t

    def accumulate(h, s, kb, diagonal):
        hs = slice(h * FOX_HEAD_DIM, (h + 1) * FOX_HEAD_DIM)
        par = kb % 2
        for r0, r1, width in row_parts(diagonal):
            for r in range(r0, r1, FOX_ROW_CHUNK):
                rows = slice(r, r + FOX_ROW_CHUNK)
                t = t_ref[par, s, rows, :width]
                m_new = jnp.broadcast_to(
                    jnp.max(t, axis=-1, keepdims=True), (FOX_ROW_CHUNK, LANES))
                if kb > 0:
                    m_old = m_ref[s, rows, :]
                    m_new = jnp.maximum(m_old, m_new)
                    alpha_ref[s, rows, :] = jnp.exp2(m_old - m_new)
                m_ref[s, rows, :] = m_new
                p_ref[par, s, rows, :width] = jnp.exp2(
                    t - jnp.tile(m_new, (1, width // LANES))).astype(BF16)
            keys = slice(kb * TK, kb * TK + width)
            v_ext = jnp.concatenate([v_ref[keys, hs], ones[:width]], axis=1)
            pv = _dot(p_ref[par, s, r0:r1, :width], v_ext)
            if kb > 0:
                pv += jnp.tile(alpha_ref[s, r0:r1, :], (1, 2)) * acc_ref[s, r0:r1, :]
            acc_ref[s, r0:r1, :] = pv

    def q_tile(n):
        for h0 in range(0, FOX_HEADS, FOX_HEADS_PER_STEP):
            heads = list(enumerate(range(h0, h0 + FOX_HEADS_PER_STEP)))
            for s, h in heads:
                logits(h, s, 0, n == 0)
            for kb in range(n + 1):
                if kb < n:
                    for s, h in heads:
                        logits(h, s, kb + 1, kb + 1 == n)
                for s, h in heads:
                    accumulate(h, s, kb, kb == n)
            for s, h in heads:
                hs = slice(h * FOX_HEAD_DIM, (h + 1) * FOX_HEAD_DIM)
                o_ref[:, hs] = (acc_ref[s, :, :FOX_HEAD_DIM]
                                / acc_ref[s, :, FOX_HEAD_DIM:]).astype(BF16)

    for n in range(k_ref.shape[0] // TQ):
        pl.when(qi == n)(functools.partial(q_tile, n))


def _fox(z, batch, seq):
    n = z.shape[0]
    d = D_MODEL
    nq = seq // TQ
    return pl.pallas_call(
        _fox_kernel,
        grid=(batch, nq),
        in_specs=[
            pl.BlockSpec((TQ, d), lambda b, q: (b * nq + q, G_FQ)),
            pl.BlockSpec((TQ, LANES), lambda b, q: (b * nq + q, DEC_COL0 // LANES)),
            pl.BlockSpec((seq, d), lambda b, q: (b, G_FK)),
            pl.BlockSpec((seq, LANES), lambda b, q: (b, DEC_COL0 // LANES + 1)),
            pl.BlockSpec((seq, d), lambda b, q: (b, G_FV)),
        ],
        out_specs=pl.BlockSpec((TQ, d), lambda b, q: (b * nq + q, 0)),
        out_shape=jax.ShapeDtypeStruct((n, d), BF16),
        scratch_shapes=[
            pltpu.VMEM((2, FOX_HEADS_PER_STEP, TQ, TK), F32),
            pltpu.VMEM((2, FOX_HEADS_PER_STEP, TQ, TK), BF16),
            pltpu.VMEM((FOX_HEADS_PER_STEP, TQ, LANES), F32),
            pltpu.VMEM((FOX_HEADS_PER_STEP, TQ, LANES), F32),
            pltpu.VMEM((FOX_HEADS_PER_STEP, TQ, 2 * FOX_HEAD_DIM), F32),
            pltpu.VMEM((FOX_HEADS, seq, LANES), BF16),
        ],
        compiler_params=pltpu.CompilerParams(
            dimension_semantics=("arbitrary", "arbitrary"), vmem_limit_bytes=VMEM_LIMIT),
        name="fox",
    )(z, z, z, z, z)


def _branch_kernel(yc_ref, xq_ref, ga_ref, gb_ref, gc_ref, yf_ref, x_ref, mk_ref, mv_ref,
                   wa_ref, wb_ref, wc_ref, wo_ref, o_ref):
    merged = ga_ref[...].astype(F32) * _dot(yc_ref[...], wa_ref[...])

    merged += gb_ref[...].astype(F32) * _dot(yf_ref[...], wb_ref[...])

    heads = []
    for h in range(XA_HEADS):
        hs = slice(h * XA_HEAD_DIM, (h + 1) * XA_HEAD_DIM)
        s = _dot_nt(xq_ref[:, hs], mk_ref[0, :, hs])
        p = jnp.exp2(s - jnp.max(s, axis=-1, keepdims=True))
        o = _dot(p.astype(BF16), mv_ref[0, :, hs])
        heads.append(o / jnp.sum(p, axis=-1, keepdims=True))
    y_xa = jnp.concatenate(heads, axis=-1).astype(BF16)
    merged += gc_ref[...].astype(F32) * _dot(y_xa, wc_ref[...])

    o_ref[...] = x_ref[...] + _dot(merged.astype(BF16), wo_ref[...])


def _branch(z, y_fox, x2, mk, mv, wa, wb, wc, wo, seq):
    n, d = x2.shape
    tm = TM_BR
    nt = n // tm
    tps = seq // tm
    n_mem = mk.shape[1]

    def zspec(g):
        return pl.BlockSpec((tm, d), lambda i: (i, g))

    return pl.pallas_call(
        _branch_kernel,
        grid=(nt,),
        in_specs=[
            zspec(G_YC), zspec(G_XQ), zspec(G_GA), zspec(G_GB), zspec(G_GC),
            pl.BlockSpec((tm, d), lambda i: (i, 0)),
            pl.BlockSpec((tm, d), lambda i: (i, 0)),
            pl.BlockSpec((1, n_mem, d), lambda i: (i // tps, 0, 0)),
            pl.BlockSpec((1, n_mem, d), lambda i: (i // tps, 0, 0)),
            _const_spec((d, d)), _const_spec((d, d)), _const_spec((d, d)), _const_spec((d, d)),
        ],
        out_specs=pl.BlockSpec((tm, d), lambda i: (i, 0)),
        out_shape=jax.ShapeDtypeStruct((n, d), F32),
        compiler_params=pltpu.CompilerParams(
            dimension_semantics=("arbitrary",), vmem_limit_bytes=VMEM_LIMIT),
        name="branch",
    )(z, z, z, z, z, y_fox, x2, mk, mv, wa, wb, wc, wo)


def _ffn_kernel(x_ref, g_ref, wi_ref, wo_ref, o_ref, a_ref):
    x = x_ref[...]
    h = _rms(x, g_ref[...]).astype(BF16)
    for c in range(D_FF // FF_CHUNK):
        g = _dot(h, wi_ref[:, c * FF_CHUNK:(c + 1) * FF_CHUNK])
        u = _dot(h, wi_ref[:, D_FF + c * FF_CHUNK:D_FF + (c + 1) * FF_CHUNK])
        a_ref[:, c * FF_CHUNK:(c + 1) * FF_CHUNK] = (g * jax.nn.sigmoid(g) * u).astype(BF16)
    o_ref[...] = x + _dot(a_ref[...], wo_ref[...])


def _ffn(x1, g, wi, wo):
    n, d = x1.shape
    tm = TM_FFN
    return pl.pallas_call(
        _ffn_kernel,
        grid=(n // tm,),
        in_specs=[
            pl.BlockSpec((tm, d), lambda i: (i, 0)),
            _const_spec((1, d)),
            _const_spec((d, 2 * D_FF)),
            _const_spec((D_FF, d)),
        ],
        out_specs=pl.BlockSpec((tm, d), lambda i: (i, 0)),
        out_shape=jax.ShapeDtypeStruct((n, d), F32),
        scratch_shapes=[pltpu.VMEM((tm, D_FF), BF16)],
        compiler_params=pltpu.CompilerParams(
            dimension_semantics=("arbitrary",), vmem_limit_bytes=VMEM_LIMIT),
        name="ffn",
    )(x1, g, wi, wo)


def kernel(x, mem, norm1_g, w_in, conv_w, conv_b, fox_f_bias, fox_q_g, fox_k_g, mem_norm_g,
           w_mem_kv, xa_q_g, xa_k_g, w_br_conv, w_br_fox, w_br_xa, w_o, norm2_g, w_ffn_in,
           w_ffn_out):
    batch, seq, d = x.shape
    depth = norm1_g.shape[0]
    n_main = N_GROUPS * d
    x2 = x.reshape(batch * seq, d)
    for l in range(depth):
        w_main = w_in[l].astype(BF16)
        w_ff = jnp.pad(w_main[:, n_main:], ((0, 0), (0, LANES - FOX_HEADS)))
        f_bias = jnp.pad(fox_f_bias[l], (0, LANES - FOX_HEADS)).reshape(1, LANES)

        mk, mv = _mem_kv(mem, mem_norm_g[l].reshape(1, d), w_mem_kv[l].astype(BF16),
                         xa_k_g[l].reshape(1, XA_HEAD_DIM))
        z = _in_proj(x2, norm1_g[l].reshape(1, d), w_main, w_ff, f_bias,
                     fox_q_g[l].reshape(1, FOX_HEAD_DIM), fox_k_g[l].reshape(1, FOX_HEAD_DIM),
                     xa_q_g[l].reshape(1, XA_HEAD_DIM), conv_w[l], conv_b[l].reshape(1, d), seq)
        y_fox = _fox(z, batch, seq)
        x1 = _branch(z, y_fox, x2, mk, mv,
                     w_br_conv[l].astype(BF16), w_br_fox[l].astype(BF16),
                     w_br_xa[l].astype(BF16), w_o[l].astype(BF16), seq)
        x2 = _ffn(x1, norm2_g[l].reshape(1, d), w_ffn_in[l].astype(BF16),
                  w_ffn_out[l].astype(BF16))
    return x2.reshape(batch, seq, d)
```

```python
import functools
import math

import jax
import jax.numpy as jnp
from jax import lax
from jax.experimental import pallas as pl
from jax.experimental.pallas import tpu as pltpu

F32 = jnp.float32
BF16 = jnp.bfloat16

D_MODEL = 1024
N_GROUPS = 10
FOX_HEADS = 8
FOX_HEAD_DIM = 128
XA_HEADS = 4
XA_HEAD_DIM = 256
D_FF = 2816
FF_CHUNK = 256
CONV_WIDTH = 3
EPS = 1e-6
LOG2E = math.log2(math.e)
LANES = 128
CONV_HALO = 8

W_CB, W_CC, W_CV, W_FQ, W_FK, W_FV, W_XQ, W_GA, W_GB, W_GC = range(N_GROUPS)
G_YC, G_FQ, G_FK, G_FV, G_XQ, G_GA, G_GB, G_GC = range(N_GROUPS - 2)
DEC_COL0 = (N_GROUPS - 2) * D_MODEL
Z_WIDTH = DEC_COL0 + 2 * 128
DEC_PARTS = 3
DEC_K0 = DEC_PARTS * FOX_HEADS

TM_IN = 512
IN_CHUNK = 256
TQ = 512
TK = 512
FOX_HEADS_PER_STEP = 2
FOX_ROW_CHUNK = 64
TM_BR = 512
TM_FFN = 1024
VMEM_LIMIT = 56 * 1024 * 1024
VMEM_LIMIT_IN_PROJ = 60 * 1024 * 1024


def _rms(x, g):
    ms = jnp.mean(x * x, axis=-1, keepdims=True)
    return x * lax.rsqrt(ms + EPS) * g


def _dot(a, b):
    return jnp.dot(a, b, preferred_element_type=F32)


def _dot_nt(a, b):
    return lax.dot_general(a, b, (((1,), (1,)), ((), ())), preferred_element_type=F32)


def _const_spec(shape):
    nd = len(shape)
    return pl.BlockSpec(shape, lambda *_: (0,) * nd, pipeline_mode=pl.Buffered(1))


def _mem_kv_kernel(mem_ref, g_ref, w_ref, kg_ref, k_ref, v_ref):
    mn = _rms(mem_ref[0], g_ref[...]).astype(BF16)
    kv = _dot(mn, w_ref[...])
    for h in range(XA_HEADS):
        hs = slice(h * XA_HEAD_DIM, (h + 1) * XA_HEAD_DIM)
        k_ref[0, :, hs] = _rms(kv[:, hs], kg_ref[...]).astype(BF16)
    v_ref[0] = kv[:, D_MODEL:].astype(BF16)


def _mem_kv(mem, g, w, kg):
    b, m, d = mem.shape
    return pl.pallas_call(
        _mem_kv_kernel,
        grid=(b,),
        in_specs=[
            pl.BlockSpec((1, m, d), lambda i: (i, 0, 0)),
            _const_spec((1, d)),
            _const_spec((d, 2 * d)),
            _const_spec((1, XA_HEAD_DIM)),
        ],
        out_specs=[
            pl.BlockSpec((1, m, d), lambda i: (i, 0, 0)),
            pl.BlockSpec((1, m, d), lambda i: (i, 0, 0)),
        ],
        out_shape=[jax.ShapeDtypeStruct((b, m, d), BF16)] * 2,
        compiler_params=pltpu.CompilerParams(
            dimension_semantics=("arbitrary",), vmem_limit_bytes=VMEM_LIMIT),
        name="mem_kv",
    )(mem, g, w, kg)


def _log_sigmoid(x):
    return jnp.minimum(x, 0.0) - jnp.log1p(jnp.exp(-jnp.abs(x)))


def _cumsum_rows(x):
    n = x.shape[0]
    row = lax.broadcasted_iota(jnp.int32, x.shape, 0)
    s = 1
    while s < n:
        x = x + jnp.where(row >= s, pltpu.roll(x, s, 0), 0.0)
        s *= 2
    return x


def _in_proj_kernel(tiles_per_seq, x_ref, g1_ref, w_ref, wff_ref, fb_ref, qg_ref, kg_ref,
                    xqg_ref, psel_ref, cw_ref, cbias_ref, z_ref,
                    carry_ref, h_ref, ucarry_ref):
    i = pl.program_id(0)
    tm = x_ref.shape[0]
    d = x_ref.shape[1]
    h_ref[...] = _rms(x_ref[...], g1_ref[...]).astype(BF16)

    lf = _log_sigmoid(_dot(h_ref[...], wff_ref[...]) + fb_ref[...])

    @pl.when(i % tiles_per_seq == 0)
    def _():
        carry_ref[...] = jnp.zeros_like(carry_ref)
        ucarry_ref[...] = jnp.zeros_like(ucarry_ref)

    c = _cumsum_rows(lf) + carry_ref[...]
    carry_ref[...] = c[tm - 1:tm, :]
    c = c * LOG2E

    hi = c.astype(BF16)
    mid = (c - hi.astype(F32)).astype(BF16)
    lo = (c - hi.astype(F32) - mid.astype(F32)).astype(BF16)
    dec = _dot(jnp.concatenate([hi, mid, lo], axis=1), psel_ref[...])
    lane = lax.broadcasted_iota(jnp.int32, (tm, LANES), 1)
    q_ones = jnp.where((lane >= DEC_K0) & (lane < 2 * DEC_K0), 1.0, 0.0)
    z_ref[:, DEC_COL0:DEC_COL0 + LANES] = (dec[:, :LANES] + q_ones).astype(BF16)
    z_ref[:, DEC_COL0 + LANES:DEC_COL0 + 2 * LANES] = dec[:, LANES:].astype(BF16)

    def head_norm(r, g_ref, scale, col0):
        hd = g_ref.shape[1]
        for h in range(r.shape[1] // hd):
            z_ref[:, col0 + h * hd:col0 + (h + 1) * hd] = (
                _rms(r[:, h * hd:(h + 1) * hd], g_ref[...]) * scale).astype(BF16)

    hrow = lax.broadcasted_iota(jnp.int32, (CONV_HALO, IN_CHUNK), 0)

    def gated_conv(cb, cc, cv, cs):
        u = cc * cv
        prev = ucarry_ref[:, cs]
        ucarry_ref[:, cs] = u[tm - CONV_HALO:, :]
        u1 = pltpu.roll(u, 1, 0)
        u2 = pltpu.roll(u, 2, 0)
        u1 = jnp.concatenate(
            [jnp.where(hrow < 1, pltpu.roll(prev, 1, 0), u1[:CONV_HALO]), u1[CONV_HALO:]], axis=0)
        u2 = jnp.concatenate(
            [jnp.where(hrow < 2, pltpu.roll(prev, 2, 0), u2[:CONV_HALO]), u2[CONV_HALO:]], axis=0)
        conv = cw_ref[0:1, cs] * u2 + cw_ref[1:2, cs] * u1 + cw_ref[2:3, cs] * u
        return cb * (conv + cbias_ref[:, cs])

    def proj(col0):
        return _dot(h_ref[...], w_ref[:, col0:col0 + IN_CHUNK])

    for c0 in range(0, d, IN_CHUNK):
        cs = slice(c0, c0 + IN_CHUNK)
        y = gated_conv(proj(W_CB * d + c0), proj(W_CC * d + c0), proj(W_CV * d + c0), cs)
        z_ref[:, G_YC * d + c0:G_YC * d + c0 + IN_CHUNK] = y.astype(BF16)

    n_chunks = N_GROUPS * d // IN_CHUNK
    for ci in range((W_CV + 1) * d // IN_CHUNK, n_chunks):
        col0 = ci * IN_CHUNK
        wg = col0 // d
        zcol0 = col0 - (W_CV - G_YC) * d
        r = proj(col0)
        if wg == W_FQ:
            head_norm(r, qg_ref, LOG2E / math.sqrt(FOX_HEAD_DIM), zcol0)
        elif wg == W_FK:
            head_norm(r, kg_ref, 1.0, zcol0)
        elif wg == W_XQ:
            head_norm(r, xqg_ref, LOG2E / math.sqrt(XA_HEAD_DIM), zcol0)
        elif wg in (W_GA, W_GB, W_GC):
            z_ref[:, zcol0:zcol0 + IN_CHUNK] = jax.nn.sigmoid(r).astype(BF16)
        else:
            z_ref[:, zcol0:zcol0 + IN_CHUNK] = r.astype(BF16)


def _decay_selector():
    sel = [[0.0] * (2 * LANES) for _ in range(DEC_PARTS * LANES)]
    for h in range(FOX_HEADS):
        for j in range(DEC_PARTS):
            sel[j * LANES + h][DEC_PARTS * h + j] = 1.0
            sel[j * LANES + h][LANES + DEC_K0 + DEC_PARTS * h + j] = -1.0
    return jnp.array(sel, BF16)


def _in_proj(x2, g1, w, wff, fb, qg, kg, xqg, cw, cbias, seq):
    n, d = x2.shape
    tm = TM_IN
    kern = functools.partial(_in_proj_kernel, seq // tm)
    psel = _decay_selector()
    return pl.pallas_call(
        kern,
        grid=(n // tm,),
        in_specs=[
            pl.BlockSpec((tm, d), lambda i: (i, 0)),
            _const_spec((1, d)),
            _const_spec(w.shape),
            _const_spec((d, LANES)),
            _const_spec((1, LANES)),
            _const_spec((1, FOX_HEAD_DIM)),
            _const_spec((1, FOX_HEAD_DIM)),
            _const_spec((1, XA_HEAD_DIM)),
            _const_spec(psel.shape),
            _const_spec((CONV_WIDTH, d)),
            _const_spec((1, d)),
        ],
        out_specs=pl.BlockSpec((tm, Z_WIDTH), lambda i: (i, 0)),
        out_shape=jax.ShapeDtypeStruct((n, Z_WIDTH), BF16),
        scratch_shapes=[
            pltpu.VMEM((1, LANES), F32),
            pltpu.VMEM((tm, d), BF16),
            pltpu.VMEM((CONV_HALO, d), F32),
        ],
        compiler_params=pltpu.CompilerParams(
            dimension_semantics=("arbitrary",), vmem_limit_bytes=VMEM_LIMIT_IN_PROJ),
        name="in_proj",
    )(x2, g1, w, wff, fb, qg, kg, xqg, psel, cw, cbias)


def _fox_kernel(q_ref, qx_ref, k_ref, kx_ref, v_ref, o_ref,
                t_ref, p_ref, m_ref, acc_ref, kxh_ref):
    qi = pl.program_id(1)
    row = lax.broadcasted_iota(jnp.int32, (TQ, TK), 0)
    col = lax.broadcasted_iota(jnp.int32, (TQ, TK), 1)
    causal = col <= row
    ones = jnp.ones((TK, FOX_HEAD_DIM), BF16)

    @pl.when(qi == 0)
    def _():
        lane = lax.broadcasted_iota(jnp.int32, kx_ref.shape, 1)
        kx = kx_ref[...]
        for h in range(FOX_HEADS):
            own_c = (lane >= DEC_K0 + DEC_PARTS * h) & (lane < DEC_K0 + DEC_PARTS * (h + 1))
            pick_q = (lane >= DEC_PARTS * h) & (lane < DEC_PARTS * (h + 1))
            kxh_ref[h] = jnp.where(own_c, kx, jnp.where(pick_q, 1.0, 0.0).astype(BF16))

    def row_parts(diagonal):
        if diagonal:
            return ((0, TQ // 2, TK // 2), (TQ // 2, TQ, TK))
        return ((0, TQ, TK),)

    def logits(h, s, kb, diagonal):
        hs = slice(h * FOX_HEAD_DIM, (h + 1) * FOX_HEAD_DIM)
        for r0, r1, width in row_parts(diagonal):
            keys = slice(kb * TK, kb * TK + width)
            t = _dot_nt(jnp.concatenate([q_ref[r0:r1, hs], qx_ref[r0:r1, :]], axis=1),
                        jnp.concatenate([k_ref[keys, hs], kxh_ref[h, keys, :]], axis=1))
            if diagonal:
                t = jnp.where(causal[r0:r1, :width], t, -jnp.inf)
            t_ref[kb % 2, s, r0:r1, :width] = t

    def accumulate(h, s, kb, diagonal):
        hs = slice(h * FOX_HEAD_DIM, (h + 1) * FOX_HEAD_DIM)
        par = kb % 2
        for r0, r1, width in row_parts(diagonal):
            alphas = []
            for r in range(r0, r1, FOX_ROW_CHUNK):
                rows = slice(r, r + FOX_ROW_CHUNK)
                t = t_ref[par, s, rows, :width]
                m_new = jnp.broadcast_to(
                    jnp.max(t, axis=-1, keepdims=True), (FOX_ROW_CHUNK, LANES))
                if kb > 0:
                    m_old = m_ref[s, rows, :]
                    m_new = jnp.maximum(m_old, m_new)
                    alphas.append(jnp.exp2(m_old - m_new))
                m_ref[s, rows, :] = m_new
                p_ref[par, s, rows, :width] = jnp.exp2(
                    t - jnp.tile(m_new, (1, width // LANES))).astype(BF16)
            keys = slice(kb * TK, kb * TK + width)
            v_ext = jnp.concatenate([v_ref[keys, hs], ones[:width]], axis=1)
            pv = _dot(p_ref[par, s, r0:r1, :width], v_ext)
            if kb > 0:
                alpha = jnp.concatenate(alphas, axis=0)
                pv += jnp.tile(alpha, (1, 2)) * acc_ref[s, r0:r1, :]
            acc_ref[s, r0:r1, :] = pv

    def q_tile(n):
        for h0 in range(0, FOX_HEADS, FOX_HEADS_PER_STEP):
            heads = list(enumerate(range(h0, h0 + FOX_HEADS_PER_STEP)))
            for s, h in heads:
                logits(h, s, 0, n == 0)
            for kb in range(n + 1):
                if kb < n:
                    for s, h in heads:
                        logits(h, s, kb + 1, kb + 1 == n)
                for s, h in heads:
                    accumulate(h, s, kb, kb == n)
            for s, h in heads:
                hs = slice(h * FOX_HEAD_DIM, (h + 1) * FOX_HEAD_DIM)
                o_ref[:, hs] = (acc_ref[s, :, :FOX_HEAD_DIM]
                                / acc_ref[s, :, FOX_HEAD_DIM:]).astype(BF16)

    for n in range(k_ref.shape[0] // TQ):
        pl.when(qi == n)(functools.partial(q_tile, n))


def _fox(z, batch, seq):
    n = z.shape[0]
    d = D_MODEL
    nq = seq // TQ
    return pl.pallas_call(
        _fox_kernel,
        grid=(batch, nq),
        in_specs=[
            pl.BlockSpec((TQ, d), lambda b, q: (b * nq + q, G_FQ)),
            pl.BlockSpec((TQ, LANES), lambda b, q: (b * nq + q, DEC_COL0 // LANES)),
            pl.BlockSpec((seq, d), lambda b, q: (b, G_FK)),
            pl.BlockSpec((seq, LANES), lambda b, q: (b, DEC_COL0 // LANES + 1)),
            pl.BlockSpec((seq, d), lambda b, q: (b, G_FV)),
        ],
        out_specs=pl.BlockSpec((TQ, d), lambda b, q: (b * nq + q, 0)),
        out_shape=jax.ShapeDtypeStruct((n, d), BF16),
        scratch_shapes=[
            pltpu.VMEM((2, FOX_HEADS_PER_STEP, TQ, TK), F32),
            pltpu.VMEM((2, FOX_HEADS_PER_STEP, TQ, TK), BF16),
            pltpu.VMEM((FOX_HEADS_PER_STEP, TQ, LANES), F32),
            pltpu.VMEM((FOX_HEADS_PER_STEP, TQ, 2 * FOX_HEAD_DIM), F32),
            pltpu.VMEM((FOX_HEADS, seq, LANES), BF16),
        ],
        compiler_params=pltpu.CompilerParams(
            dimension_semantics=("arbitrary", "arbitrary"), vmem_limit_bytes=VMEM_LIMIT),
        name="fox",
    )(z, z, z, z, z)


def _branch_kernel(yc_ref, xq_ref, ga_ref, gb_ref, gc_ref, yf_ref, x_ref, mk_ref, mv_ref,
                   wa_ref, wb_ref, wc_ref, wo_ref, o_ref):
    merged = ga_ref[...].astype(F32) * _dot(yc_ref[...], wa_ref[...])

    merged += gb_ref[...].astype(F32) * _dot(yf_ref[...], wb_ref[...])

    heads = []
    for h in range(XA_HEADS):
        hs = slice(h * XA_HEAD_DIM, (h + 1) * XA_HEAD_DIM)
        s = _dot_nt(xq_ref[:, hs], mk_ref[0, :, hs])
        p = jnp.exp2(s - jnp.max(s, axis=-1, keepdims=True))
        o = _dot(p.astype(BF16), mv_ref[0, :, hs])
        heads.append(o / jnp.sum(p, axis=-1, keepdims=True))
    y_xa = jnp.concatenate(heads, axis=-1).astype(BF16)
    merged += gc_ref[...].astype(F32) * _dot(y_xa, wc_ref[...])

    o_ref[...] = x_ref[...] + _dot(merged.astype(BF16), wo_ref[...])


def _branch(z, y_fox, x2, mk, mv, wa, wb, wc, wo, seq):
    n, d = x2.shape
    tm = TM_BR
    nt = n // tm
    tps = seq // tm
    n_mem = mk.shape[1]

    def zspec(g):
        return pl.BlockSpec((tm, d), lambda i: (i, g))

    return pl.pallas_call(
        _branch_kernel,
        grid=(nt,),
        in_specs=[
            zspec(G_YC), zspec(G_XQ), zspec(G_GA), zspec(G_GB), zspec(G_GC),
            pl.BlockSpec((tm, d), lambda i: (i, 0)),
            pl.BlockSpec((tm, d), lambda i: (i, 0)),
            pl.BlockSpec((1, n_mem, d), lambda i: (i // tps, 0, 0)),
            pl.BlockSpec((1, n_mem, d), lambda i: (i // tps, 0, 0)),
            _const_spec((d, d)), _const_spec((d, d)), _const_spec((d, d)), _const_spec((d, d)),
        ],
        out_specs=pl.BlockSpec((tm, d), lambda i: (i, 0)),
        out_shape=jax.ShapeDtypeStruct((n, d), F32),
        compiler_params=pltpu.CompilerParams(
            dimension_semantics=("arbitrary",), vmem_limit_bytes=VMEM_LIMIT),
        name="branch",
    )(z, z, z, z, z, y_fox, x2, mk, mv, wa, wb, wc, wo)


def _ffn_kernel(x_ref, g_ref, wi_ref, wo_ref, o_ref, a_ref):
    x = x_ref[...]
    h = _rms(x, g_ref[...]).astype(BF16)
    for c in range(D_FF // FF_CHUNK):
        g = _dot(h, wi_ref[:, c * FF_CHUNK:(c + 1) * FF_CHUNK])
        u = _dot(h, wi_ref[:, D_FF + c * FF_CHUNK:D_FF + (c + 1) * FF_CHUNK])
        a_ref[:, c * FF_CHUNK:(c + 1) * FF_CHUNK] = (g * jax.nn.sigmoid(g) * u).astype(BF16)
    o_ref[...] = x + _dot(a_ref[...], wo_ref[...])


def _ffn(x1, g, wi, wo):
    n, d = x1.shape
    tm = TM_FFN
    return pl.pallas_call(
        _ffn_kernel,
        grid=(n // tm,),
        in_specs=[
            pl.BlockSpec((tm, d), lambda i: (i, 0)),
            _const_spec((1, d)),
            _const_spec((d, 2 * D_FF)),
            _const_spec((D_FF, d)),
        ],
        out_specs=pl.BlockSpec((tm, d), lambda i: (i, 0)),
        out_shape=jax.ShapeDtypeStruct((n, d), F32),
        scratch_shapes=[pltpu.VMEM((tm, D_FF), BF16)],
        compiler_params=pltpu.CompilerParams(
            dimension_semantics=("arbitrary",), vmem_limit_bytes=VMEM_LIMIT),
        name="ffn",
    )(x1, g, wi, wo)


def kernel(x, mem, norm1_g, w_in, conv_w, conv_b, fox_f_bias, fox_q_g, fox_k_g, mem_norm_g,
           w_mem_kv, xa_q_g, xa_k_g, w_br_conv, w_br_fox, w_br_xa, w_o, norm2_g, w_ffn_in,
           w_ffn_out):
    batch, seq, d = x.shape
    depth = norm1_g.shape[0]
    n_main = N_GROUPS * d
    x2 = x.reshape(batch * seq, d)
    for l in range(depth):
        w_main = w_in[l].astype(BF16)
        w_ff = jnp.pad(w_main[:, n_main:], ((0, 0), (0, LANES - FOX_HEADS)))
        f_bias = jnp.pad(fox_f_bias[l], (0, LANES - FOX_HEADS)).reshape(1, LANES)

        mk, mv = _mem_kv(mem, mem_norm_g[l].reshape(1, d), w_mem_kv[l].astype(BF16),
                         xa_k_g[l].reshape(1, XA_HEAD_DIM))
        z = _in_proj(x2, norm1_g[l].reshape(1, d), w_main, w_ff, f_bias,
                     fox_q_g[l].reshape(1, FOX_HEAD_DIM), fox_k_g[l].reshape(1, FOX_HEAD_DIM),
                     xa_q_g[l].reshape(1, XA_HEAD_DIM), conv_w[l], conv_b[l].reshape(1, d), seq)
        y_fox = _fox(z, batch, seq)
        x1 = _branch(z, y_fox, x2, mk, mv,
                     w_br_conv[l].astype(BF16), w_br_fox[l].astype(BF16),
                     w_br_xa[l].astype(BF16), w_o[l].astype(BF16), seq)
        x2 = _ffn(x1, norm2_g[l].reshape(1, d), w_ffn_in[l].astype(BF16),
                  w_ffn_out[l].astype(BF16))
    return x2.reshape(batch, seq, d)
```
